```python
import math
import jax, jax.numpy as jnp
from jax import lax
import numpy as np

D_MODEL = 1024
BATCH = 8
SEQ = 4096
DEPTH = 2
DEC_BATCH = 32
DEC_SEQ = 32
PAST_LEN = 4096

CHUNK = 64
BAND_CHUNKS = 8
WINDOW_A = BAND_CHUNKS * CHUNK
HA = 8
DHA = 64
HB = 4
DHB = 64
HC = 4
DHC = 128
N_MEM = 256
REL_CLIP = 128
D_FF = 2816
QBLOCK = 128
D_MIX = HA * DHA + HB * 2 * DHB
D_IN = 3 * HA * DHA + 3 * HB * 2 * DHB
EPS = 1e-6
NEG_INF = -1e30

kernel_name = 'streaming_hybrid_band_diff_encoder'


def rmsnorm(x, g):
    xf = x.astype(jnp.float32)
    y = xf * lax.rsqrt(jnp.mean(xf * xf, axis=-1, keepdims=True) + EPS)
    return (y * g.astype(jnp.float32)).astype(x.dtype)


def swiglu(x, w_gu, w_dn):
    gate, up = jnp.split(x @ w_gu, 2, axis=-1)
    return (jax.nn.silu(gate) * up) @ w_dn


def ffn_half(x, g_pre, g_post, w_gu, w_dn):
    return x + 0.5 * rmsnorm(swiglu(rmsnorm(x, g_pre), w_gu, w_dn), g_post)


def alibi_slopes():
    return jnp.asarray(2.0 ** (-8.0 * np.arange(1, HB + 1) / HB), dtype=jnp.float32)


def rel_bias_lookup(table, rel):
    idx = jnp.clip(rel, -REL_CLIP, REL_CLIP) + REL_CLIP
    return table[:, idx].astype(jnp.float32)


def band_attention_prompt(q, k, v, rel_table):
    B, T = q.shape[:2]
    nc = T // CHUNK
    band = (BAND_CHUNKS + 1) * CHUNK
    qc = q.reshape(B, nc, CHUNK, HA, DHA)
    pad = ((0, 0), (WINDOW_A, 0), (0, 0), (0, 0))
    kp = jnp.pad(k, pad).reshape(B, nc + BAND_CHUNKS, CHUNK, HA, DHA)
    vp = jnp.pad(v, pad).reshape(B, nc + BAND_CHUNKS, CHUNK, HA, DHA)
    kb = jnp.concatenate([kp[:, o:o + nc] for o in range(BAND_CHUNKS + 1)], axis=2)
    vb = jnp.concatenate([vp[:, o:o + nc] for o in range(BAND_CHUNKS + 1)], axis=2)
    i = jnp.arange(CHUNK)
    j = jnp.arange(band)
    bias = rel_bias_lookup(rel_table, i[:, None] - (j[None, :] - WINDOW_A))
    k_pos = (jnp.arange(nc)[:, None] - BAND_CHUNKS) * CHUNK + j[None, :]
    s = jnp.einsum('bcqhd,bckhd->bchqk', qc, kb).astype(jnp.float32) * DHA ** -0.5 + bias
    s = jnp.where((k_pos >= 0)[None, :, None, None, :], s, NEG_INF)
    p = jax.nn.softmax(s, axis=-1)
    o = jnp.einsum('bchqk,bckhd->bcqhd', p.astype(vb.dtype), vb)
    return o.reshape(B, T, HA, DHA)


def band_attention_sample(q, k, v, cache_k, cache_v, past_len, rel_table):
    W = cache_k.shape[1]
    S = q.shape[1]
    k_all = jnp.concatenate([cache_k, k], axis=1)
    v_all = jnp.concatenate([cache_v, v], axis=1)
    q_pos = past_len + jnp.arange(S)
    k_pos = past_len - W + jnp.arange(W + S)
    bias = rel_bias_lookup(rel_table, q_pos[:, None] - k_pos[None, :])
    s = jnp.einsum('bqhd,bkhd->bhqk', q, k_all).astype(jnp.float32) * DHA ** -0.5 + bias
    p = jax.nn.softmax(s, axis=-1)
    return jnp.einsum('bhqk,bkhd->bqhd', p.astype(v_all.dtype), v_all)


def diff_attention(q, k, v, q_pos, k_pos, lam):
    B, Tq = q.shape[:2]
    qb = min(QBLOCK, Tq)
    nb = Tq // qb
    q_blocks = jnp.moveaxis(q.reshape(B, nb, qb, HB, 2, DHB), 1, 0)
    pos_blocks = q_pos.reshape(nb, qb)
    k_chunk = k_pos // CHUNK
    slopes = alibi_slopes()

    def attend_block(args):
        qblk, qp = args
        s = jnp.einsum('bqhid,bkhid->bihqk', qblk, k).astype(jnp.float32) * DHB ** -0.5
        dist = jnp.abs(qp[:, None] - k_pos[None, :]).astype(jnp.float32)
        s = s - slopes[:, None, None] * dist
        visible = k_chunk[None, :] <= (qp // CHUNK)[:, None]
        s = jnp.where(visible, s, NEG_INF)
        p = jax.nn.softmax(s, axis=-1)
        a = p[:, 0] - lam * p[:, 1]
        return jnp.einsum('bhqk,bkhe->bqhe', a.astype(v.dtype), v)

    o = lax.map(attend_block, (q_blocks, pos_blocks))
    return jnp.moveaxis(o, 0, 1).reshape(B, Tq, HB, 2 * DHB)


def cross_attend(q, mem_k, mem_v):
    s = jnp.einsum('bthd,bmhd->bhtm', q, mem_k).astype(jnp.float32) * DHC ** -0.5
    p = jax.nn.softmax(s, axis=-1)
    return jnp.einsum('bhtm,bmhd->bthd', p.astype(mem_v.dtype), mem_v)


def memory_kv(mem, g, w_ckv):
    B = mem.shape[0]
    mk, mv = jnp.split(rmsnorm(mem, g) @ w_ckv, 2, axis=-1)
    return mk.reshape(B, N_MEM, HC, DHC), mv.reshape(B, N_MEM, HC, DHC)


def trunk_layer(l, x, mem_k, mem_v, cache, norm_g, w_ffn_gu, w_ffn_dn, w_in, rel_bias,
                diff_lambda, subln_g, w_out, w_cq, w_co):
    g = norm_g[l]
    x = ffn_half(x, g[0], g[1], w_ffn_gu[l, 0], w_ffn_dn[l, 0])
    u = rmsnorm(x, g[2])
    B, T = u.shape[:2]
    qa, ka, va, qb, kb, vb = jnp.split(u @ w_in[l], 6, axis=-1)
    qa = qa.reshape(B, T, HA, DHA)
    ka = ka.reshape(B, T, HA, DHA)
    va = va.reshape(B, T, HA, DHA)
    qb = qb.reshape(B, T, HB, 2, DHB)
    kb = kb.reshape(B, T, HB, 2, DHB)
    vb = vb.reshape(B, T, HB, 2 * DHB)
    lam_init = 0.8 - 0.6 * math.exp(-0.3 * l)
    lp = diff_lambda[l].astype(jnp.float32)
    lam = jnp.exp(jnp.sum(lp[0] * lp[1])) - jnp.exp(jnp.sum(lp[2] * lp[3])) + lam_init
    if cache is None:
        oa = band_attention_prompt(qa, ka, va, rel_bias[l])
        pos = jnp.arange(T)
        ob = diff_attention(qb, kb, vb, pos, pos, lam)
        keep = min(WINDOW_A, T)
        new_state = (ka[:, T - keep:], va[:, T - keep:], kb, vb)
    else:
        ca_k, ca_v, cb_k, cb_v = cache
        past_len = cb_k.shape[1]
        oa = band_attention_sample(qa, ka, va, ca_k, ca_v, past_len, rel_bias[l])
        kb_all = jnp.concatenate([cb_k, kb], axis=1)
        vb_all = jnp.concatenate([cb_v, vb], axis=1)
        ob = diff_attention(qb, kb_all, vb_all, past_len + jnp.arange(T), jnp.arange(past_len + T), lam)
        new_state = (ka, va, kb, vb)
    ob = rmsnorm(ob, subln_g[l]) * (1.0 - lam_init)
    mix = jnp.concatenate([oa.reshape(B, T, HA * DHA), ob.reshape(B, T, HB * 2 * DHB)], axis=-1) @ w_out[l]
    x = x + rmsnorm(mix, g[3])
    qc = (rmsnorm(x, g[4]) @ w_cq[l]).reshape(B, T, HC, DHC)
    oc = cross_attend(qc, mem_k, mem_v).reshape(B, T, HC * DHC) @ w_co[l]
    x = x + rmsnorm(oc, g[5])
    x = ffn_half(x, g[6], g[7], w_ffn_gu[l, 1], w_ffn_dn[l, 1])
    return x, new_state


def setup_inputs(seed: int = 0) -> dict:
    key = jax.random.key(seed)
    ks = jax.random.split(key, 21)

    def nrm(k, shape, scale):
        return jax.random.normal(k, shape, jnp.float32) * scale

    wa = min(WINDOW_A, PAST_LEN)
    return {
        'x_prompt': nrm(ks[0], (BATCH, SEQ, D_MODEL), 1.0),
        'x_sample': nrm(ks[1], (DEC_BATCH, DEC_SEQ, D_MODEL), 1.0),
        'mem_prompt': nrm(ks[2], (BATCH, N_MEM, D_MODEL), 1.0),
        'cache_a_k': nrm(ks[3], (DEPTH, DEC_BATCH, wa, HA, DHA), 1.0),
        'cache_a_v': nrm(ks[4], (DEPTH, DEC_BATCH, wa, HA, DHA), 1.0),
        'cache_b_k': nrm(ks[5], (DEPTH, DEC_BATCH, PAST_LEN, HB, 2, DHB), 1.0),
        'cache_b_v': nrm(ks[6], (DEPTH, DEC_BATCH, PAST_LEN, HB, 2 * DHB), 1.0),
        'cache_mem_k': nrm(ks[7], (DEPTH, DEC_BATCH, N_MEM, HC, DHC), 1.0),
        'cache_mem_v': nrm(ks[8], (DEPTH, DEC_BATCH, N_MEM, HC, DHC), 1.0),
        'norm_g': 1.0 + nrm(ks[9], (DEPTH, 8, D_MODEL), 0.05),
        'w_ffn_gu': nrm(ks[10], (DEPTH, 2, D_MODEL, 2 * D_FF), D_MODEL ** -0.5),
        'w_ffn_dn': nrm(ks[11], (DEPTH, 2, D_FF, D_MODEL), D_FF ** -0.5),
        'w_in': nrm(ks[12], (DEPTH, D_MODEL, D_IN), D_MODEL ** -0.5),
        'rel_bias': nrm(ks[13], (DEPTH, HA, 2 * REL_CLIP + 1), 0.1),
        'diff_lambda': nrm(ks[14], (DEPTH, 4, DHB), 0.1),
        'subln_g': 1.0 + nrm(ks[15], (DEPTH, 2 * DHB), 0.05),
        'w_out': nrm(ks[16], (DEPTH, D_MIX, D_MODEL), D_MIX ** -0.5),
        'mem_norm_g': 1.0 + nrm(ks[17], (DEPTH, D_MODEL), 0.05),
        'w_cq': nrm(ks[18], (DEPTH, D_MODEL, HC * DHC), D_MODEL ** -0.5),
        'w_ckv': nrm(ks[19], (DEPTH, D_MODEL, 2 * HC * DHC), D_MODEL ** -0.5),
        'w_co': nrm(ks[20], (DEPTH, HC * DHC, D_MODEL), (HC * DHC) ** -0.5),
    }


def reference(x_prompt, x_sample, mem_prompt, cache_a_k, cache_a_v, cache_b_k, cache_b_v,
              cache_mem_k, cache_mem_v, norm_g, w_ffn_gu, w_ffn_dn, w_in, rel_bias, diff_lambda,
              subln_g, w_out, mem_norm_g, w_cq, w_ckv, w_co):
    xp = x_prompt
    xs = x_sample
    p_ak, p_av, p_bk, p_bv, p_mk, p_mv = [], [], [], [], [], []
    s_ak, s_av, s_bk, s_bv = [], [], [], []
    for l in range(DEPTH):
        mk, mv = memory_kv(mem_prompt, mem_norm_g[l], w_ckv[l])
        xp, st_p = trunk_layer(l, xp, mk, mv, None, norm_g, w_ffn_gu, w_ffn_dn, w_in, rel_bias,
                               diff_lambda, subln_g, w_out, w_cq, w_co)
        xs, st_s = trunk_layer(l, xs, cache_mem_k[l], cache_mem_v[l],
                               (cache_a_k[l], cache_a_v[l], cache_b_k[l], cache_b_v[l]),
                               norm_g, w_ffn_gu, w_ffn_dn, w_in, rel_bias,
                               diff_lambda, subln_g, w_out, w_cq, w_co)
        p_ak.append(st_p[0]); p_av.append(st_p[1]); p_bk.append(st_p[2]); p_bv.append(st_p[3])
        p_mk.append(mk); p_mv.append(mv)
        s_ak.append(st_s[0]); s_av.append(st_s[1]); s_bk.append(st_s[2]); s_bv.append(st_s[3])
    return (xp, xs,
            jnp.stack(p_ak), jnp.stack(p_av), jnp.stack(p_bk), jnp.stack(p_bv),
            jnp.stack(p_mk), jnp.stack(p_mv),
            jnp.stack(s_ak), jnp.stack(s_av), jnp.stack(s_bk), jnp.stack(s_bv))
```

```python
import functools
import math

import jax
import jax.numpy as jnp
from jax import lax
from jax.experimental import pallas as pl
from jax.experimental.pallas import tpu as pltpu

F32 = jnp.float32
BF16 = jnp.bfloat16

CHUNK = 64
BAND_CHUNKS = 8
WINDOW_A = BAND_CHUNKS * CHUNK
BAND = WINDOW_A + CHUNK
HA, DHA = 8, 64
HB, DHB = 4, 64
HC, DHC = 4, 128
REL_CLIP = 128
EPS = 1e-6
NEG_INF = -1e30

LANES = 128
VMEM_LIMIT = 56 * 1024 * 1024
ROW_TILE = 512
FF_CHUNK = 256
DIFF_TQ = 256
DIFF_TK = 256


def _params(n_axes):
    return pltpu.CompilerParams(
        dimension_semantics=("arbitrary",) * n_axes, vmem_limit_bytes=VMEM_LIMIT)


def _rms(x, g):
    ms = jnp.mean(x * x, axis=-1, keepdims=True)
    return x * lax.rsqrt(ms + EPS) * g


def _dot(a, b):
    return jnp.dot(a, b, preferred_element_type=F32)


def _dot_nt(a, b):
    return lax.dot_general(a, b, (((1,), (1,)), ((), ())), preferred_element_type=F32)


def _chunk_of(pos):
    assert CHUNK & (CHUNK - 1) == 0
    return lax.shift_right_logical(pos, jnp.int32(CHUNK.bit_length() - 1))


def _resident(shape):
    zeros = (0,) * len(shape)
    return pl.BlockSpec(shape, lambda *_: zeros, pipeline_mode=pl.Buffered(1))


def _ffn_kernel(x_ref, gpre_ref, gpost_ref, wgu_ref, wdn_ref, o_ref, acc_ref, *, d_ff):
    x = x_ref[...]
    xn = _rms(x, gpre_ref[...]).astype(BF16)
    for c in range(d_ff // FF_CHUNK):
        lo = c * FF_CHUNK
        gate = _dot(xn, wgu_ref[:, lo:lo + FF_CHUNK])
        up = _dot(xn, wgu_ref[:, d_ff + lo:d_ff + lo + FF_CHUNK])
        hidden = (gate * jax.nn.sigmoid(gate) * up).astype(BF16)
        part = _dot(hidden, wdn_ref[lo:lo + FF_CHUNK, :])
        if c == 0:
            acc_ref[...] = part
        else:
            acc_ref[...] += part
    o_ref[...] = x + 0.5 * _rms(acc_ref[...], gpost_ref[...])


def _ffn_half(x, g_pre, g_post, w_gu, w_dn):
    m, d = x.shape
    d_ff = w_dn.shape[0]
    assert d_ff % FF_CHUNK == 0 and m % ROW_TILE == 0
    row = pl.BlockSpec((ROW_TILE, d), lambda i: (i, 0))
    return pl.pallas_call(
        functools.partial(_ffn_kernel, d_ff=d_ff),
        grid=(m // ROW_TILE,),
        in_specs=[row, _resident((1, d)), _resident((1, d)),
                  _resident((d, 2 * d_ff)), _resident((d_ff, d))],
        out_specs=row,
        out_shape=jax.ShapeDtypeStruct((m, d), F32),
        scratch_shapes=[pltpu.VMEM((ROW_TILE, d), F32)],
        compiler_params=_params(1),
        name="ffn_half",
    )(x, g_pre, g_post, w_gu, w_dn)


def _proj_kernel(x_ref, g_ref, w_ref, qa_ref, ka_ref, va_ref, qb_ref, kb_ref, vb_ref, *, width):
    u = _rms(x_ref[...], g_ref[...]).astype(BF16)
    outs = (qa_ref, ka_ref, va_ref, qb_ref, kb_ref, vb_ref)
    scales = (DHA ** -0.5, None, None, DHB ** -0.5, None, None)
    for n, (o_ref, scale) in enumerate(zip(outs, scales)):
        y = _dot(u, w_ref[:, n * width:(n + 1) * width])
        o_ref[...] = y if scale is None else y * scale


def _in_proj(x, g, w_in):
    m, d = x.shape
    width = w_in.shape[1] // 6
    row = pl.BlockSpec((ROW_TILE, d), lambda i: (i, 0))
    out = pl.BlockSpec((ROW_TILE, width), lambda i: (i, 0))
    return pl.pallas_call(
        functools.partial(_proj_kernel, width=width),
        grid=(m // ROW_TILE,),
        in_specs=[row, _resident((1, d)), _resident(w_in.shape)],
        out_specs=[out] * 6,
        out_shape=[jax.ShapeDtypeStruct((m, width), F32)] * 6,
        compiler_params=_params(1),
        name="in_proj",
    )(x, g, w_in)


def _memkv_kernel(x_ref, g_ref, w_ref, k_ref, v_ref, *, width):
    u = _rms(x_ref[...], g_ref[...]).astype(BF16)
    k_ref[...] = _dot(u, w_ref[:, :width])
    v_ref[...] = _dot(u, w_ref[:, width:])


def _memory_kv(mem, g, w_ckv):
    m, d = mem.shape
    width = w_ckv.shape[1] // 2
    row = pl.BlockSpec((ROW_TILE, d), lambda i: (i, 0))
    out = pl.BlockSpec((ROW_TILE, width), lambda i: (i, 0))
    return pl.pallas_call(
        functools.partial(_memkv_kernel, width=width),
        grid=(m // ROW_TILE,),
        in_specs=[row, _resident((1, d)), _resident(w_ckv.shape)],
        out_specs=[out] * 2,
        out_shape=[jax.ShapeDtypeStruct((m, width), F32)] * 2,
        compiler_params=_params(1),
        name="memory_kv",
    )(mem, g, w_ckv)


def _bias_kernel(table_ref, o_ref):
    h = pl.program_id(0)
    i = lax.broadcasted_iota(jnp.int32, (CHUNK, BAND), 0)
    j = lax.broadcasted_iota(jnp.int32, (CHUNK, BAND), 1)
    idx = jnp.clip(i - (j - WINDOW_A), -REL_CLIP, REL_CLIP) + REL_CLIP

    def pick(r, acc):
        return jnp.where(idx == r, table_ref[h, r], acc)

    o_ref[0] = lax.fori_loop(0, 2 * REL_CLIP + 1, pick, jnp.zeros((CHUNK, BAND), F32))


def _band_bias(table):
    return pl.pallas_call(
        _bias_kernel,
        grid=(HA,),
        in_specs=[pl.BlockSpec(memory_space=pltpu.SMEM)],
        out_specs=pl.BlockSpec((1, CHUNK, BAND), lambda h: (h, 0, 0)),
        out_shape=jax.ShapeDtypeStruct((HA, CHUNK, BAND), F32),
        compiler_params=_params(1),
        name="band_bias",
    )(table)


def _split_heads(x):
    lane = lax.broadcasted_iota(jnp.int32, x.shape, 1)
    lo = jnp.where(lane < LANES // 2, x, 0.0).astype(BF16)
    hi = jnp.where(lane >= LANES // 2, x, 0.0).astype(BF16)
    return lo, hi


def _softmax_pv(s, v):
    m = jnp.max(s, axis=-1, keepdims=True)
    p = jnp.exp(s - m)
    l = jnp.sum(p, axis=-1, keepdims=True)
    return _dot(p.astype(BF16), v) / l


def _band_prompt_kernel(q_ref, kp_ref, kc_ref, vp_ref, vc_ref, bias_ref, o_ref, kcat_ref, vcat_ref):
    t = pl.program_id(1)
    tile = kc_ref.shape[1]
    kcat_ref[0:tile, :] = kp_ref[0].astype(BF16)
    kcat_ref[tile:2 * tile, :] = kc_ref[0].astype(BF16)
    vcat_ref[0:tile, :] = vp_ref[0].astype(BF16)
    vcat_ref[tile:2 * tile, :] = vc_ref[0].astype(BF16)
    col = lax.broadcasted_iota(jnp.int32, (CHUNK, BAND), 1)
    out_lane = lax.broadcasted_iota(jnp.int32, (CHUNK, LANES), 1)

    for pair in range(HA // 2):
        lanes = slice(pair * LANES, (pair + 1) * LANES)

        def chunk_body(c, carry, lanes=lanes, pair=pair):
            r0 = pl.multiple_of(c * CHUNK, CHUNK)
            q_lo, q_hi = _split_heads(q_ref[0, pl.ds(r0, CHUNK), lanes])
            k = kcat_ref[pl.ds(r0, BAND), lanes]
            v = vcat_ref[pl.ds(r0, BAND), lanes]
            valid = (col >= WINDOW_A - r0) | (t > 0)
            s_lo = jnp.where(valid, _dot_nt(q_lo, k) + bias_ref[2 * pair], NEG_INF)
            s_hi = jnp.where(valid, _dot_nt(q_hi, k) + bias_ref[2 * pair + 1], NEG_INF)
            o = jnp.where(out_lane < LANES // 2, _softmax_pv(s_lo, v), _softmax_pv(s_hi, v))
            o_ref[0, pl.ds(r0, CHUNK), lanes] = o.astype(o_ref.dtype)
            return carry

        lax.fori_loop(0, tile // CHUNK, chunk_body, 0)


def _band_prompt(q, k, v, bias):
    b, t, w = q.shape
    tile = WINDOW_A
    assert t % tile == 0
    cur = pl.BlockSpec((1, tile, w), lambda i, j: (i, j, 0))
    prev = pl.BlockSpec((1, tile, w), lambda i, j: (i, jnp.maximum(j - 1, 0), 0))
    return pl.pallas_call(
        _band_prompt_kernel,
        grid=(b, t // tile),
        in_specs=[cur, prev, cur, prev, cur, _resident(bias.shape)],
        out_specs=cur,
        out_shape=jax.ShapeDtypeStruct((b, t, w), BF16),
        scratch_shapes=[pltpu.VMEM((2 * tile, w), BF16), pltpu.VMEM((2 * tile, w), BF16)],
        compiler_params=_params(2),
        name="band_prompt",
    )(q, k, k, v, v, bias)


def _band_sample_kernel(q_ref, kc_ref, kn_ref, vc_ref, vn_ref, bias_ref, o_ref, kcat_ref, vcat_ref):
    past = kc_ref.shape[1]
    s_len = q_ref.shape[1]
    kcat_ref[0:past, :] = kc_ref[0].astype(BF16)
    kcat_ref[past:past + s_len, :] = kn_ref[0].astype(BF16)
    vcat_ref[0:past, :] = vc_ref[0].astype(BF16)
    vcat_ref[past:past + s_len, :] = vn_ref[0].astype(BF16)
    out_lane = lax.broadcasted_iota(jnp.int32, (s_len, LANES), 1)
    for pair in range(HA // 2):
        lanes = slice(pair * LANES, (pair + 1) * LANES)
        q_lo, q_hi = _split_heads(q_ref[0, :, lanes])
        k = kcat_ref[:, lanes]
        v = vcat_ref[:, lanes]
        s_lo = _dot_nt(q_lo, k) + bias_ref[2 * pair, 0:s_len, 0:past + s_len]
        s_hi = _dot_nt(q_hi, k) + bias_ref[2 * pair + 1, 0:s_len, 0:past + s_len]
        o = jnp.where(out_lane < LANES // 2, _softmax_pv(s_lo, v), _softmax_pv(s_hi, v))
        o_ref[0, :, lanes] = o.astype(o_ref.dtype)


def _band_sample(q, k, v, cache_k, cache_v, bias):
    b, s_len, w = q.shape
    past = cache_k.shape[1]
    assert past == WINDOW_A and s_len <= CHUNK
    new = pl.BlockSpec((1, s_len, w), lambda i: (i, 0, 0))
    old = pl.BlockSpec((1, past, w), lambda i: (i, 0, 0))
    return pl.pallas_call(
        _band_sample_kernel,
        grid=(b,),
        in_specs=[new, old, new, old, new, _resident(bias.shape)],
        out_specs=new,
        out_shape=jax.ShapeDtypeStruct((b, s_len, w), BF16),
        scratch_shapes=[pltpu.VMEM((past + s_len, w), BF16), pltpu.VMEM((past + s_len, w), BF16)],
        compiler_params=_params(1),
        name="band_sample",
    )(q, cache_k, k, cache_v, v, bias)


def _diff_lambda(dl_ref, lam_init):
    lp = dl_ref[...]
    a = jnp.sum(lp[0:1] * lp[1:2], axis=-1, keepdims=True)
    b = jnp.sum(lp[2:3] * lp[3:4], axis=-1, keepdims=True)
    return jnp.exp(a) - jnp.exp(b) + lam_init


def _diff_finish(acc, l, lam, lam_init, sg_ref, rows):
    o = acc / l
    o = o[:rows] - lam * o[rows:]
    return _rms(o, sg_ref[...]) * (1.0 - lam_init)


def _diff_prompt_kernel(slopes_ref, consts_ref, dl_ref, sg_ref, q_ref, k_ref, v_ref, o_ref):
    h = pl.program_id(1)
    qi = pl.program_id(2)
    tq, tk = DIFF_TQ, DIFF_TK
    slope = slopes_ref[h]
    lam_init = consts_ref[0]
    lam = _diff_lambda(dl_ref, lam_init)
    q_lo, q_hi = _split_heads(q_ref[0])
    q2 = jnp.concatenate([q_lo, q_hi], axis=0)

    def load(j):
        off = pl.multiple_of(j * tk, tk)
        k = k_ref[0, pl.ds(off, tk), :].astype(BF16)
        v = v_ref[0, pl.ds(off, tk), :].astype(BF16)
        return _dot_nt(q2, k), v

    def update(s, v, carry):
        m, l, acc = carry
        m_new = jnp.maximum(m, jnp.max(s, axis=-1, keepdims=True))
        alpha = jnp.exp(m - m_new)
        p = jnp.exp(s - m_new)
        l = alpha * l + jnp.sum(p, axis=-1, keepdims=True)
        acc = alpha * acc + _dot(p.astype(BF16), v)
        return m_new, l, acc

    kcol = lax.broadcasted_iota(jnp.int32, (1, tk), 1)

    def past_body(j, carry):
        s, v = load(j)
        kpos = (j * tk + kcol).astype(F32)
        return update(s + slope * kpos, v, carry)

    init = (jnp.full((2 * tq, 1), NEG_INF, F32), jnp.zeros((2 * tq, 1), F32),
            jnp.zeros((2 * tq, 2 * DHB), F32))
    carry = lax.fori_loop(0, qi, past_body, init)

    row = lax.broadcasted_iota(jnp.int32, (2 * tq, 1), 0)
    qloc = jnp.where(row >= tq, row - tq, row)
    s, v = load(qi)
    rel = qloc - kcol
    bias = slope * (qi * tq + qloc - jnp.abs(rel)).astype(F32)
    visible = _chunk_of(kcol) <= _chunk_of(qloc)
    _, l, acc = update(jnp.where(visible, s + bias, NEG_INF), v, carry)
    o_ref[0] = _diff_finish(acc, l, lam, lam_init, sg_ref, tq).astype(o_ref.dtype)


def _diff_prompt(q, k, v, slopes, consts, diff_lambda, subln_g):
    b, t, w = q.shape
    hw = 2 * DHB
    assert t % DIFF_TQ == 0 and DIFF_TQ == DIFF_TK and DIFF_TQ % CHUNK == 0
    smem = pl.BlockSpec(memory_space=pltpu.SMEM)
    qspec = pl.BlockSpec((1, DIFF_TQ, hw), lambda i, h, j: (i, j, h))
    kvspec = pl.BlockSpec((1, t, hw), lambda i, h, j: (i, 0, h))
    return pl.pallas_call(
        _diff_prompt_kernel,
        grid=(b, HB, t // DIFF_TQ),
        in_specs=[smem, smem, _resident(diff_lambda.shape), _resident(subln_g.shape),
                  qspec, kvspec, kvspec],
        out_specs=qspec,
        out_shape=jax.ShapeDtypeStruct((b, t, w), BF16),
        compiler_params=_params(3),
        name="diff_prompt",
    )(slopes, consts, diff_lambda, subln_g, q, k, v)


def _diff_sample_kernel(slopes_ref, consts_ref, dl_ref, sg_ref, q_ref, kc_ref, kn_ref, vc_ref, vn_ref,
                        o_ref):
    h = pl.program_id(1)
    past = kc_ref.shape[1]
    s_len = q_ref.shape[1]
    slope = slopes_ref[h]
    lam_init = consts_ref[0]
    lam = _diff_lambda(dl_ref, lam_init)
    q_lo, q_hi = _split_heads(q_ref[0])
    q2 = jnp.concatenate([q_lo, q_hi], axis=0)

    kpos = lax.broadcasted_iota(jnp.int32, (1, past), 1).astype(F32)
    s_old = _dot_nt(q2, kc_ref[0].astype(BF16)) + slope * kpos
    row = lax.broadcasted_iota(jnp.int32, (2 * s_len, 1), 0)
    qloc = jnp.where(row >= s_len, row - s_len, row)
    kcol = lax.broadcasted_iota(jnp.int32, (1, s_len), 1)
    bias = slope * (past + qloc - jnp.abs(qloc - kcol)).astype(F32)
    s_new = _dot_nt(q2, kn_ref[0].astype(BF16)) + bias

    m = jnp.maximum(jnp.max(s_old, axis=-1, keepdims=True), jnp.max(s_new, axis=-1, keepdims=True))
    p_old = jnp.exp(s_old - m)
    p_new = jnp.exp(s_new - m)
    l = jnp.sum(p_old, axis=-1, keepdims=True) + jnp.sum(p_new, axis=-1, keepdims=True)
    acc = _dot(p_old.astype(BF16), vc_ref[0].astype(BF16)) + _dot(p_new.astype(BF16), vn_ref[0].astype(BF16))
    o_ref[0] = _diff_finish(acc, l, lam, lam_init, sg_ref, s_len).astype(o_ref.dtype)


def _diff_sample(q, k, v, cache_k, cache_v, slopes, consts, diff_lambda, subln_g):
    b, s_len, w = q.shape
    past = cache_k.shape[1]
    hw = 2 * DHB
    assert past % CHUNK + s_len <= CHUNK
    smem = pl.BlockSpec(memory_space=pltpu.SMEM)
    new = pl.BlockSpec((1, s_len, hw), lambda i, h: (i, 0, h))
    old = pl.BlockSpec((1, past, hw), lambda i, h: (i, 0, h))
    return pl.pallas_call(
        _diff_sample_kernel,
        grid=(b, HB),
        in_specs=[smem, smem, _resident(diff_lambda.shape), _resident(subln_g.shape),
                  new, old, new, old, new],
        out_specs=new,
        out_shape=jax.ShapeDtypeStruct((b, s_len, w), BF16),
        compiler_params=_params(2),
        name="diff_sample",
    )(slopes, consts, diff_lambda, subln_g, q, cache_k, k, cache_v, v)


def _mix_kernel(x_ref, oa_ref, ob_ref, mk_ref, mv_ref, g_ref, wout_ref, wcq_ref, wco_ref, o_ref):
    x = x_ref[0]
    wa = oa_ref.shape[2]
    mix = _dot(oa_ref[0], wout_ref[:wa, :]) + _dot(ob_ref[0], wout_ref[wa:, :])
    x = x + _rms(mix, g_ref[0:1])
    qc = _dot(_rms(x, g_ref[1:2]).astype(BF16), wcq_ref[...]).astype(BF16)
    heads = []
    for h in range(HC):
        lanes = slice(h * DHC, (h + 1) * DHC)
        s = _dot_nt(qc[:, lanes], mk_ref[0, :, lanes].astype(BF16)) * DHC ** -0.5
        heads.append(_softmax_pv(s, mv_ref[0, :, lanes].astype(BF16)).astype(BF16))
    oc = _dot(jnp.concatenate(heads, axis=-1), wco_ref[...])
    o_ref[0] = x + _rms(oc, g_ref[2:3])


def _mix_cross(x, oa, ob, mem_k, mem_v, g345, w_out, w_cq, w_co):
    b, t, d = x.shape
    tm = min(ROW_TILE, t)
    assert t % tm == 0
    n_mem, wm = mem_k.shape[1:]

    def rows(width):
        return pl.BlockSpec((1, tm, width), lambda i, j: (i, j, 0))

    mem = pl.BlockSpec((1, n_mem, wm), lambda i, j: (i, 0, 0))
    return pl.pallas_call(
        _mix_kernel,
        grid=(b, t // tm),
        in_specs=[rows(d), rows(oa.shape[2]), rows(ob.shape[2]), mem, mem, _resident(g345.shape),
                  _resident(w_out.shape), _resident(w_cq.shape), _resident(w_co.shape)],
        out_specs=rows(d),
        out_shape=jax.ShapeDtypeStruct((b, t, d), F32),
        compiler_params=_params(2),
        name="mix_cross",
    )(x, oa, ob, mem_k, mem_v, g345, w_out, w_cq, w_co)


def _trunk_layer(x, mem_k, mem_v, cache, lw):
    b, t, d = x.shape
    g = lw["norm_g"]
    x2 = _ffn_half(x.reshape(b * t, d), g[0:1], g[1:2], lw["w_gu"][0], lw["w_dn"][0])
    qa, ka, va, qb, kb, vb = (y.reshape(b, t, -1) for y in _in_proj(x2, g[2:3], lw["w_in"]))
    if cache is None:
        oa = _band_prompt(qa, ka, va, lw["bias"])
        ob = _diff_prompt(qb, kb, vb, lw["slopes"], lw["consts"], lw["diff_lambda"], lw["subln_g"])
        keep = min(WINDOW_A, t)
        state = (ka[:, t - keep:], va[:, t - keep:], kb, vb)
    else:
        ca_k, ca_v, cb_k, cb_v = cache
        oa = _band_sample(qa, ka, va, ca_k, ca_v, lw["bias"])
        ob = _diff_sample(qb, kb, vb, cb_k, cb_v, lw["slopes"], lw["consts"], lw["diff_lambda"],
                          lw["subln_g"])
        state = (ka, va, kb, vb)
    x3 = _mix_cross(x2.reshape(b, t, d), oa, ob, mem_k, mem_v, g[3:6], lw["w_out"], lw["w_cq"],
                    lw["w_co"])
    x4 = _ffn_half(x3.reshape(b * t, d), g[6:7], g[7:8], lw["w_gu"][1], lw["w_dn"][1])
    return x4.reshape(b, t, d), state


def kernel(x_prompt, x_sample, mem_prompt, cache_a_k, cache_a_v, cache_b_k, cache_b_v, cache_mem_k,
           cache_mem_v, norm_g, w_ffn_gu, w_ffn_dn, w_in, rel_bias, diff_lambda, subln_g, w_out,
           mem_norm_g, w_cq, w_ckv, w_co):
    depth = norm_g.shape[0]
    b, t, d = x_prompt.shape
    bd, sd, _ = x_sample.shape
    n_mem = mem_prompt.shape[1]
    slopes = jnp.asarray([2.0 ** (-8.0 * (i + 1) / HB) for i in range(HB)], F32)
    flat = lambda a: a.reshape(a.shape[0], a.shape[1], -1)

    xp, xs = x_prompt, x_sample
    p_state, s_state, p_mem = [], [], []
    for l in range(depth):
        lam_init = 0.8 - 0.6 * math.exp(-0.3 * l)
        lw = dict(
            norm_g=norm_g[l], w_gu=w_ffn_gu[l].astype(BF16), w_dn=w_ffn_dn[l].astype(BF16),
            w_in=w_in[l].astype(BF16), w_out=w_out[l].astype(BF16), w_cq=w_cq[l].astype(BF16),
            w_co=w_co[l].astype(BF16), bias=_band_bias(rel_bias[l]), slopes=slopes,
            consts=jnp.asarray([lam_init], F32), diff_lambda=diff_lambda[l],
            subln_g=subln_g[l].reshape(1, -1))
        mk, mv = _memory_kv(mem_prompt.reshape(b * n_mem, d), mem_norm_g[l].reshape(1, d),
                            w_ckv[l].astype(BF16))
        mk, mv = mk.reshape(b, n_mem, -1), mv.reshape(b, n_mem, -1)
        xp, st_p = _trunk_layer(xp, mk, mv, None, lw)
        cache = (flat(cache_a_k[l]), flat(cache_a_v[l]), flat(cache_b_k[l]), flat(cache_b_v[l]))
        xs, st_s = _trunk_layer(xs, flat(cache_mem_k[l]), flat(cache_mem_v[l]), cache, lw)
        p_state.append(st_p)
        s_state.append(st_s)
        p_mem.append((mk, mv))

    def stack(items, idx, shape):
        return jnp.stack([it[idx] for it in items]).reshape((depth,) + shape)

    keep = min(WINDOW_A, t)
    return (xp, xs,
            stack(p_state, 0, (b, keep, HA, DHA)), stack(p_state, 1, (b, keep, HA, DHA)),
            stack(p_state, 2, (b, t, HB, 2, DHB)), stack(p_state, 3, (b, t, HB, 2 * DHB)),
            stack(p_mem, 0, (b, n_mem, HC, DHC)), stack(p_mem, 1, (b, n_mem, HC, DHC)),
            stack(s_state, 0, (bd, sd, HA, DHA)), stack(s_state, 1, (bd, sd, HA, DHA)),
            stack(s_state, 2, (bd, sd, HB, 2, DHB)), stack(s_state, 3, (bd, sd, HB, 2 * DHB)))
```

```python
import functools
import math

import jax
import jax.numpy as jnp
import numpy as np
from jax import lax
from jax.experimental import pallas as pl
from jax.experimental.pallas import tpu as pltpu

F32 = jnp.float32
BF16 = jnp.bfloat16

CHUNK = 64
BAND_CHUNKS = 8
WINDOW_A = BAND_CHUNKS * CHUNK
BAND = WINDOW_A + CHUNK
HA, DHA = 8, 64
HB, DHB = 4, 64
HC, DHC = 4, 128
REL_CLIP = 128
EPS = 1e-6
NEG_INF = -1e30
LOG2E = math.log2(math.e)
ALIBI_SLOPES = tuple(2.0 ** (-8.0 * (i + 1) / HB) for i in range(HB))

LANES = 128
VMEM_LIMIT = 56 * 1024 * 1024
ROW_TILE = 512
FF_CHUNK = 256
BAND_UNROLL = 2
DIFF_TQ = 256
DIFF_TK = 256
ONES_ROWS = 16
N_SLOPE_TERMS = 4


def _params(n_axes):
    return pltpu.CompilerParams(
        dimension_semantics=("arbitrary",) * n_axes, vmem_limit_bytes=VMEM_LIMIT)


def _rms(x, g):
    ms = jnp.mean(x * x, axis=-1, keepdims=True)
    return x * lax.rsqrt(ms + EPS) * g


def _dot(a, b):
    return jnp.dot(a, b, preferred_element_type=F32)


def _dot_nt(a, b):
    return lax.dot_general(a, b, (((1,), (1,)), ((), ())), preferred_element_type=F32)


def _chunk_of(pos):
    assert CHUNK & (CHUNK - 1) == 0
    return lax.shift_right_logical(pos, jnp.int32(CHUNK.bit_length() - 1))


def _resident(shape):
    zeros = (0,) * len(shape)
    return pl.BlockSpec(shape, lambda *_: zeros, pipeline_mode=pl.Buffered(1))


def _ffn_kernel(x_ref, gpre_ref, gpost_ref, wgu_ref, wdn_ref, o_ref, acc_ref, *, d_ff):
    x = x_ref[...]
    xn = _rms(x, gpre_ref[...]).astype(BF16)
    for c in range(d_ff // FF_CHUNK):
        lo = c * FF_CHUNK
        gate = _dot(xn, wgu_ref[:, lo:lo + FF_CHUNK])
        up = _dot(xn, wgu_ref[:, d_ff + lo:d_ff + lo + FF_CHUNK])
        hidden = (gate * jax.nn.sigmoid(gate) * up).astype(BF16)
        part = _dot(hidden, wdn_ref[lo:lo + FF_CHUNK, :])
        if c == 0:
            acc_ref[...] = part
        else:
            acc_ref[...] += part
    o_ref[...] = x + 0.5 * _rms(acc_ref[...], gpost_ref[...])


def _ffn_half(x, g_pre, g_post, w_gu, w_dn):
    m, d = x.shape
    d_ff = w_dn.shape[0]
    assert d_ff % FF_CHUNK == 0 and m % ROW_TILE == 0
    row = pl.BlockSpec((ROW_TILE, d), lambda i: (i, 0))
    return pl.pallas_call(
        functools.partial(_ffn_kernel, d_ff=d_ff),
        grid=(m // ROW_TILE,),
        in_specs=[row, _resident((1, d)), _resident((1, d)),
                  _resident((d, 2 * d_ff)), _resident((d_ff, d))],
        out_specs=row,
        out_shape=jax.ShapeDtypeStruct((m, d), F32),
        scratch_shapes=[pltpu.VMEM((ROW_TILE, d), F32)],
        compiler_params=_params(1),
        name="ffn_half",
    )(x, g_pre, g_post, w_gu, w_dn)


def _proj_kernel(x_ref, g_ref, w_ref, qa_ref, ka_ref, va_ref, qb_ref, kb_ref, vb_ref, *, width):
    u = _rms(x_ref[...], g_ref[...]).astype(BF16)
    outs = (qa_ref, ka_ref, va_ref, qb_ref, kb_ref, vb_ref)
    scales = (DHA ** -0.5 * LOG2E, None, None, DHB ** -0.5 * LOG2E, None, None)
    for n, (o_ref, scale) in enumerate(zip(outs, scales)):
        y = _dot(u, w_ref[:, n * width:(n + 1) * width])
        o_ref[...] = y if scale is None else y * scale


def _in_proj(x, g, w_in):
    m, d = x.shape
    width = w_in.shape[1] // 6
    row = pl.BlockSpec((ROW_TILE, d), lambda i: (i, 0))
    out = pl.BlockSpec((ROW_TILE, width), lambda i: (i, 0))
    return pl.pallas_call(
        functools.partial(_proj_kernel, width=width),
        grid=(m // ROW_TILE,),
        in_specs=[row, _resident((1, d)), _resident(w_in.shape)],
        out_specs=[out] * 6,
        out_shape=[jax.ShapeDtypeStruct((m, width), F32)] * 6,
        compiler_params=_params(1),
        name="in_proj",
    )(x, g, w_in)


def _memkv_kernel(x_ref, g_ref, w_ref, k_ref, v_ref, *, width):
    u = _rms(x_ref[...], g_ref[...]).astype(BF16)
    k_ref[...] = _dot(u, w_ref[:, :width])
    v_ref[...] = _dot(u, w_ref[:, width:])


def _memory_kv(mem, g, w_ckv):
    m, d = mem.shape
    width = w_ckv.shape[1] // 2
    row = pl.BlockSpec((ROW_TILE, d), lambda i: (i, 0))
    out = pl.BlockSpec((ROW_TILE, width), lambda i: (i, 0))
    return pl.pallas_call(
        functools.partial(_memkv_kernel, width=width),
        grid=(m // ROW_TILE,),
        in_specs=[row, _resident((1, d)), _resident(w_ckv.shape)],
        out_specs=[out] * 2,
        out_shape=[jax.ShapeDtypeStruct((m, width), F32)] * 2,
        compiler_params=_params(1),
        name="memory_kv",
    )(mem, g, w_ckv)


def _bias_kernel(table_ref, o_ref):
    h = pl.program_id(0)
    i = lax.broadcasted_iota(jnp.int32, (CHUNK, BAND), 0)
    j = lax.broadcasted_iota(jnp.int32, (CHUNK, BAND), 1)
    idx = jnp.clip(i - (j - WINDOW_A), -REL_CLIP, REL_CLIP) + REL_CLIP

    def pick(r, acc):
        return jnp.where(idx == r, table_ref[h, r], acc)

    o_ref[0] = lax.fori_loop(0, 2 * REL_CLIP + 1, pick, jnp.zeros((CHUNK, BAND), F32)) * LOG2E


def _band_bias(table):
    return pl.pallas_call(
        _bias_kernel,
        grid=(HA,),
        in_specs=[pl.BlockSpec(memory_space=pltpu.SMEM)],
        out_specs=pl.BlockSpec((1, CHUNK, BAND), lambda h: (h, 0, 0)),
        out_shape=jax.ShapeDtypeStruct((HA, CHUNK, BAND), F32),
        compiler_params=_params(1),
        name="band_bias",
    )(table)


def _split_heads(x):
    lane = lax.broadcasted_iota(jnp.int32, x.shape, 1)
    lo = jnp.where(lane < LANES // 2, x, 0.0).astype(BF16)
    hi = jnp.where(lane >= LANES // 2, x, 0.0).astype(BF16)
    return lo, hi


def _exp2_rows(s):
    p = jnp.exp2(s - jnp.max(s, axis=-1, keepdims=True))
    return p.astype(BF16), jnp.sum(p, axis=-1, keepdims=True)


def _band_prompt_kernel(q_ref, kp_ref, kc_ref, vp_ref, vc_ref, bias_ref, o_ref, kcat_ref, vcat_ref):
    t = pl.program_id(1)
    tile = kc_ref.shape[1]
    kcat_ref[0:tile, :] = kp_ref[0].astype(BF16)
    kcat_ref[tile:2 * tile, :] = kc_ref[0].astype(BF16)
    vcat_ref[0:tile, :] = vp_ref[0].astype(BF16)
    vcat_ref[tile:2 * tile, :] = vc_ref[0].astype(BF16)
    col = lax.broadcasted_iota(jnp.int32, (CHUNK, BAND), 1)
    out_lane = lax.broadcasted_iota(jnp.int32, (CHUNK, LANES), 1)

    def step(i, carry):
        work = []
        for sub in range(BAND_UNROLL):
            r0 = pl.multiple_of((i * BAND_UNROLL + sub) * CHUNK, CHUNK)
            valid = (col >= WINDOW_A - r0) | (t > 0)
            for pair in range(HA // 2):
                lanes = slice(pair * LANES, (pair + 1) * LANES)
                q_lo, q_hi = _split_heads(q_ref[0, pl.ds(r0, CHUNK), lanes])
                k = kcat_ref[pl.ds(r0, BAND), lanes]
                work.append((r0, pair, lanes, valid, _dot_nt(q_lo, k), _dot_nt(q_hi, k)))
        probs = []
        for r0, pair, lanes, valid, s_lo, s_hi in work:
            p_lo, l_lo = _exp2_rows(jnp.where(valid, s_lo + bias_ref[2 * pair], NEG_INF))
            p_hi, l_hi = _exp2_rows(jnp.where(valid, s_hi + bias_ref[2 * pair + 1], NEG_INF))
            probs.append((r0, lanes, p_lo, l_lo, p_hi, l_hi))
        for r0, lanes, p_lo, l_lo, p_hi, l_hi in probs:
            v = vcat_ref[pl.ds(r0, BAND), lanes]
            o = jnp.where(out_lane < LANES // 2, _dot(p_lo, v) / l_lo, _dot(p_hi, v) / l_hi)
            o_ref[0, pl.ds(r0, CHUNK), lanes] = o.astype(o_ref.dtype)
        return carry

    lax.fori_loop(0, tile // (CHUNK * BAND_UNROLL), step, 0)


def _band_prompt(q, k, v, bias):
    b, t, w = q.shape
    tile = WINDOW_A
    assert t % tile == 0
    cur = pl.BlockSpec((1, tile, w), lambda i, j: (i, j, 0))
    prev = pl.BlockSpec((1, tile, w), lambda i, j: (i, jnp.maximum(j - 1, 0), 0))
    return pl.pallas_call(
        _band_prompt_kernel,
        grid=(b, t // tile),
        in_specs=[cur, prev, cur, prev, cur, _resident(bias.shape)],
        out_specs=cur,
        out_shape=jax.ShapeDtypeStruct((b, t, w), BF16),
        scratch_shapes=[pltpu.VMEM((2 * tile, w), BF16), pltpu.VMEM((2 * tile, w), BF16)],
        compiler_params=_params(2),
        name="band_prompt",
    )(q, k, k, v, v, bias)


def _band_sample_kernel(q_ref, kt_ref, kn_ref, vt_ref, vn_ref, bias_ref, o_ref):
    past = kt_ref.shape[3]
    s_len = q_ref.shape[1]
    out_lane = lax.broadcasted_iota(jnp.int32, (s_len, LANES), 1)
    for pair in range(HA // 2):
        lanes = slice(pair * LANES, (pair + 1) * LANES)
        q_lo, q_hi = _split_heads(q_ref[0, :, lanes])
        kt = kt_ref[0, 0, lanes, :].astype(BF16)
        vt = vt_ref[0, 0, lanes, :].astype(BF16)
        kn = kn_ref[0, :, lanes].astype(BF16)
        vn = vn_ref[0, :, lanes].astype(BF16)
        outs = []
        for head, qh in ((2 * pair, q_lo), (2 * pair + 1, q_hi)):
            s_old = _dot(qh, kt) + bias_ref[head, 0:s_len, 0:past]
            s_new = _dot_nt(qh, kn) + bias_ref[head, 0:s_len, past:past + s_len]
            m = jnp.maximum(jnp.max(s_old, axis=-1, keepdims=True),
                            jnp.max(s_new, axis=-1, keepdims=True))
            p_old = jnp.exp2(s_old - m)
            p_new = jnp.exp2(s_new - m)
            l = jnp.sum(p_old, axis=-1, keepdims=True) + jnp.sum(p_new, axis=-1, keepdims=True)
            outs.append(_dot_nt((p_old / l).astype(BF16), vt) + _dot((p_new / l).astype(BF16), vn))
        o = jnp.where(out_lane < LANES // 2, outs[0], outs[1])
        o_ref[0, :, lanes] = o.astype(o_ref.dtype)


def _band_sample(layer, q, k, v, cache_kt, cache_vt, bias):
    b, s_len, w = q.shape
    past = cache_kt.shape[3]
    assert past == WINDOW_A and s_len <= CHUNK
    new = pl.BlockSpec((1, s_len, w), lambda i: (i, 0, 0))
    old = pl.BlockSpec((1, 1, w, past), lambda i: (layer, i, 0, 0))
    return pl.pallas_call(
        _band_sample_kernel,
        grid=(b,),
        in_specs=[new, old, new, old, new, _resident(bias.shape)],
        out_specs=new,
        out_shape=jax.ShapeDtypeStruct((b, s_len, w), BF16),
        compiler_params=_params(1),
        name="band_sample",
    )(q, cache_kt, k, cache_vt, v, bias)


def _diff_lambda(dl_ref, lam_init):
    lp = dl_ref[...]
    a = jnp.sum(lp[0:1] * lp[1:2], axis=-1, keepdims=True)
    b = jnp.sum(lp[2:3] * lp[3:4], axis=-1, keepdims=True)
    return jnp.exp(a) - jnp.exp(b) + lam_init


def _bf16_terms(x):
    terms, rest = [], float(x)
    for _ in range(N_SLOPE_TERMS):
        term = float(np.asarray(rest, np.float32).astype(BF16))
        terms.append(term)
        rest -= term
    return terms


def _key_position_features(kpos, lane):
    hi = lax.shift_left(_chunk_of(kpos), jnp.int32(CHUNK.bit_length() - 1))
    lo = kpos & (CHUNK - 1)
    feat = jnp.where((lane & 1) == 0, hi, lo)
    return jnp.where(lane < 2 * N_SLOPE_TERMS, feat, 0).astype(F32).astype(BF16)


def _slope_features(head, lane):
    feat = jnp.zeros(lane.shape, F32)
    for i, term in enumerate(_bf16_terms(ALIBI_SLOPES[head] * LOG2E)):
        feat = jnp.where(lax.shift_right_logical(lane, jnp.int32(1)) == i, term, feat)
    return feat.astype(BF16)


def _diff_prompt_kernel(consts_ref, dl_ref, sg_ref, q_ref, k_ref, v_ref, o_ref,
                        kaug_ref, vt_ref, acc_ref):
    qi = pl.program_id(1)
    tq, tk = DIFF_TQ, DIFF_TK
    t = k_ref.shape[1]
    lam_init = consts_ref[0]
    lam = _diff_lambda(dl_ref, lam_init)

    @pl.when(qi == 0)
    def _():
        lane = lax.broadcasted_iota(jnp.int32, (tk, LANES), 1)
        row = lax.broadcasted_iota(jnp.int32, (tk, LANES), 0)

        def build(blk, carry):
            r0 = pl.multiple_of(blk * tk, tk)
            feat = _key_position_features(r0 + row, lane)
            for h in range(HB):
                lanes = slice(h * LANES, (h + 1) * LANES)
                kaug_ref[h, pl.ds(r0, tk), 0:LANES] = k_ref[0, pl.ds(r0, tk), lanes].astype(BF16)
                kaug_ref[h, pl.ds(r0, tk), LANES:2 * LANES] = feat
                vt_ref[h, blk, 0:LANES, :] = v_ref[0, pl.ds(r0, tk), lanes].T.astype(BF16)
                vt_ref[h, blk, LANES:LANES + ONES_ROWS, :] = jnp.ones((ONES_ROWS, tk), BF16)
            return carry

        lax.fori_loop(0, t // tk, build, 0)

    q = q_ref[0]
    qlane = lax.broadcasted_iota(jnp.int32, (tq, LANES), 1)
    qd = []
    for h in range(HB):
        q_lo, q_hi = _split_heads(q[:, h * LANES:(h + 1) * LANES])
        feat = _slope_features(h, qlane)
        qd.append(jnp.concatenate([jnp.concatenate([q_lo, feat], axis=1),
                                   jnp.concatenate([q_hi, feat], axis=1)], axis=0))
        acc_ref[h] = jnp.zeros(acc_ref.shape[1:], F32)

    def scores(j):
        off = pl.multiple_of(j * tk, tk)
        return [_dot_nt(kaug_ref[h, pl.ds(off, tk), :], qd[h]) for h in range(HB)]

    def update(j, ss, ms):
        out = []
        for h in range(HB):
            m_new = jnp.maximum(ms[h], jnp.max(ss[h], axis=0, keepdims=True))
            p = jnp.exp2(ss[h] - m_new).astype(BF16)
            acc_ref[h] = jnp.exp2(ms[h] - m_new) * acc_ref[h] + _dot(vt_ref[h, j], p)
            out.append(m_new)
        return tuple(out)

    def one_block(j, ms):
        return update(j, scores(j), ms)

    def two_blocks(jj, ms):
        s0, s1 = scores(2 * jj), scores(2 * jj + 1)
        return update(2 * jj + 1, s1, update(2 * jj, s0, ms))

    jd = (qi * tq) // tk
    ms = tuple(jnp.full((1, 2 * tq), NEG_INF, F32) for _ in range(HB))
    ms = lax.fori_loop(0, jd // 2, two_blocks, ms)
    ms = lax.cond(jd % 2 == 1, lambda c: one_block(jd - 1, c), lambda c: c, ms)

    kpos = jd * tk + lax.broadcasted_iota(jnp.int32, (tk, 2 * tq), 0)
    qpos = qi * tq + (lax.broadcasted_iota(jnp.int32, (tk, 2 * tq), 1) & (tq - 1))
    ahead = jnp.maximum(kpos - qpos, 0).astype(F32)
    visible = _chunk_of(kpos) <= _chunk_of(qpos)
    ss = [jnp.where(visible, s - (2.0 * ALIBI_SLOPES[h] * LOG2E) * ahead, NEG_INF)
          for h, s in enumerate(scores(jd))]
    update(jd, ss, ms)
    for h in range(HB):
        acc = acc_ref[h, 0:LANES, :] / acc_ref[h, LANES:LANES + 1, :]
        o = (acc[:, :tq] - lam * acc[:, tq:]).T
        o = _rms(o, sg_ref[...]) * (1.0 - lam_init)
        o_ref[0, :, h * LANES:(h + 1) * LANES] = o.astype(o_ref.dtype)


def _diff_prompt(q, k, v, consts, diff_lambda, subln_g):
    b, t, w = q.shape
    tq, tk = DIFF_TQ, DIFF_TK
    assert t % tk == 0 and tk == tq and tq % CHUNK == 0 and t < 2 ** 14
    assert tq & (tq - 1) == 0
    smem = pl.BlockSpec(memory_space=pltpu.SMEM)
    qspec = pl.BlockSpec((1, tq, w), lambda i, j: (i, j, 0))
    kvspec = pl.BlockSpec((1, t, w), lambda i, j: (i, 0, 0), pipeline_mode=pl.Buffered(1))
    return pl.pallas_call(
        _diff_prompt_kernel,
        grid=(b, t // tq),
        in_specs=[smem, _resident(diff_lambda.shape), _resident(subln_g.shape), qspec, kvspec, kvspec],
        out_specs=qspec,
        out_shape=jax.ShapeDtypeStruct((b, t, w), BF16),
        scratch_shapes=[pltpu.VMEM((HB, t, 2 * LANES), BF16),
                        pltpu.VMEM((HB, t // tk, LANES + ONES_ROWS, tk), BF16),
                        pltpu.VMEM((HB, LANES + ONES_ROWS, 2 * tq), F32)],
        compiler_params=_params(2),
        name="diff_prompt",
    )(consts, diff_lambda, subln_g, q, k, v)


def _diff_sample_kernel(consts_ref, dl_ref, sg_ref, q_ref, kt_ref, kn_ref, v_ref, vn_ref, o_ref):
    past = kt_ref.shape[4]
    s_len = q_ref.shape[1]
    lam_init = consts_ref[0]
    lam = _diff_lambda(dl_ref, lam_init)
    kpos = lax.broadcasted_iota(jnp.int32, (1, past), 1).astype(F32)
    row = lax.broadcasted_iota(jnp.int32, (2 * s_len, 1), 0)
    qloc = jnp.where(row >= s_len, row - s_len, row)
    kcol = lax.broadcasted_iota(jnp.int32, (1, s_len), 1)
    near = (past + qloc - jnp.abs(qloc - kcol)).astype(F32)
    for h in range(HB):
        lanes = slice(h * LANES, (h + 1) * LANES)
        c = ALIBI_SLOPES[h] * LOG2E
        q_lo, q_hi = _split_heads(q_ref[0, :, lanes])
        q2 = jnp.concatenate([q_lo, q_hi], axis=0)
        s_old = _dot(q2, kt_ref[0, 0, h].astype(BF16)) + c * kpos
        s_new = _dot_nt(q2, kn_ref[0, :, lanes].astype(BF16)) + c * near
        m = jnp.maximum(jnp.max(s_old, axis=-1, keepdims=True),
                        jnp.max(s_new, axis=-1, keepdims=True))
        p_old = jnp.exp2(s_old - m)
        p_new = jnp.exp2(s_new - m)
        l = jnp.sum(p_old, axis=-1, keepdims=True) + jnp.sum(p_new, axis=-1, keepdims=True)
        v_old = v_ref[0, 0, pl.ds(h, past, stride=HB), :].astype(BF16)
        acc = _dot(p_old.astype(BF16), v_old) + _dot(p_new.astype(BF16), vn_ref[0, :, lanes].astype(BF16))
        o = acc / l
        o = o[:s_len] - lam * o[s_len:]
        o = _rms(o, sg_ref[...]) * (1.0 - lam_init)
        o_ref[0, :, lanes] = o.astype(o_ref.dtype)


def _diff_sample(layer, q, k, v, cache_kt, cache_v, consts, diff_lambda, subln_g):
    b, s_len, w = q.shape
    past = cache_kt.shape[4]
    assert past % CHUNK + s_len <= CHUNK
    smem = pl.BlockSpec(memory_space=pltpu.SMEM)
    new = pl.BlockSpec((1, s_len, w), lambda i: (i, 0, 0))
    old_k = pl.BlockSpec((1, 1, HB, LANES, past), lambda i: (layer, i, 0, 0, 0))
    old_v = pl.BlockSpec((1, 1, past * HB, LANES), lambda i: (layer, i, 0, 0))
    return pl.pallas_call(
        _diff_sample_kernel,
        grid=(b,),
        in_specs=[smem, _resident(diff_lambda.shape), _resident(subln_g.shape),
                  new, old_k, new, old_v, new],
        out_specs=new,
        out_shape=jax.ShapeDtypeStruct((b, s_len, w), BF16),
        compiler_params=_params(1),
        name="diff_sample",
    )(consts, diff_lambda, subln_g, q, cache_kt, k, cache_v, v)


def _softmax_pv(s, v):
    m = jnp.max(s, axis=-1, keepdims=True)
    p = jnp.exp(s - m)
    l = jnp.sum(p, axis=-1, keepdims=True)
    return _dot(p.astype(BF16), v) / l


def _mix_kernel(x_ref, oa_ref, ob_ref, mk_ref, mv_ref, g_ref, wout_ref, wcq_ref, wco_ref, o_ref):
    x = x_ref[0]
    wa = oa_ref.shape[2]
    mix = _dot(oa_ref[0], wout_ref[:wa, :]) + _dot(ob_ref[0], wout_ref[wa:, :])
    x = x + _rms(mix, g_ref[0:1])
    qc = _dot(_rms(x, g_ref[1:2]).astype(BF16), wcq_ref[...]).astype(BF16)
    heads = []
    for h in range(HC):
        lanes = slice(h * DHC, (h + 1) * DHC)
        s = _dot_nt(qc[:, lanes], mk_ref[0, :, lanes].astype(BF16)) * DHC ** -0.5
        heads.append(_softmax_pv(s, mv_ref[0, :, lanes].astype(BF16)).astype(BF16))
    oc = _dot(jnp.concatenate(heads, axis=-1), wco_ref[...])
    o_ref[0] = x + _rms(oc, g_ref[2:3])


def _mix_cross(x, oa, ob, mem_k, mem_v, g345, w_out, w_cq, w_co):
    b, t, d = x.shape
    tm = min(ROW_TILE, t)
    assert t % tm == 0
    n_mem, wm = mem_k.shape[1:]

    def rows(width):
        return pl.BlockSpec((1, tm, width), lambda i, j: (i, j, 0))

    mem = pl.BlockSpec((1, n_mem, wm), lambda i, j: (i, 0, 0))
    return pl.pallas_call(
        _mix_kernel,
        grid=(b, t // tm),
        in_specs=[rows(d), rows(oa.shape[2]), rows(ob.shape[2]), mem, mem, _resident(g345.shape),
                  _resident(w_out.shape), _resident(w_cq.shape), _resident(w_co.shape)],
        out_specs=rows(d),
        out_shape=jax.ShapeDtypeStruct((b, t, d), F32),
        compiler_params=_params(2),
        name="mix_cross",
    )(x, oa, ob, mem_k, mem_v, g345, w_out, w_cq, w_co)


def _trunk_layer(layer, x, mem_k, mem_v, cache, lw):
    b, t, d = x.shape
    g = lw["norm_g"]
    x2 = _ffn_half(x.reshape(b * t, d), g[0:1], g[1:2], lw["w_gu"][0], lw["w_dn"][0])
    qa, ka, va, qb, kb, vb = (y.reshape(b, t, -1) for y in _in_proj(x2, g[2:3], lw["w_in"]))
    if cache is None:
        oa = _band_prompt(qa, ka, va, lw["bias"])
        ob = _diff_prompt(qb, kb, vb, lw["consts"], lw["diff_lambda"], lw["subln_g"])
        keep = min(WINDOW_A, t)
        state = (ka[:, t - keep:], va[:, t - keep:], kb, vb)
    else:
        ca_kt, ca_vt, cb_kt, cb_v = cache
        oa = _band_sample(layer, qa, ka, va, ca_kt, ca_vt, lw["bias"])
        ob = _diff_sample(layer, qb, kb, vb, cb_kt, cb_v, lw["consts"], lw["diff_lambda"],
                          lw["subln_g"])
        state = (ka, va, kb, vb)
    x3 = _mix_cross(x2.reshape(b, t, d), oa, ob, mem_k, mem_v, g[3:6], lw["w_out"], lw["w_cq"],
                    lw["w_co"])
    x4 = _ffn_half(x3.reshape(b * t, d), g[6:7], g[7:8], lw["w_gu"][1], lw["w_dn"][1])
    return x4.reshape(b, t, d), state


def kernel(x_prompt, x_sample, mem_prompt, cache_a_k, cache_a_v, cache_b_k, cache_b_v, cache_mem_k,
           cache_mem_v, norm_g, w_ffn_gu, w_ffn_dn, w_in, rel_bias, diff_lambda, subln_g, w_out,
           mem_norm_g, w_cq, w_ckv, w_co):
    depth = norm_g.shape[0]
    b, t, d = x_prompt.shape
    bd, sd, _ = x_sample.shape
    n_mem = mem_prompt.shape[1]
    past_a, past_b = cache_a_k.shape[2], cache_b_k.shape[2]
    flat = lambda a: a.reshape(a.shape[0], a.shape[1], -1)

    cache = (jnp.transpose(cache_a_k, (0, 1, 3, 4, 2)).reshape(depth, bd, HA * DHA, past_a),
             jnp.transpose(cache_a_v, (0, 1, 3, 4, 2)).reshape(depth, bd, HA * DHA, past_a),
             jnp.transpose(cache_b_k, (0, 1, 3, 4, 5, 2)).reshape(depth, bd, HB, 2 * DHB, past_b),
             cache_b_v.reshape(depth, bd, past_b * HB, 2 * DHB))

    xp, xs = x_prompt, x_sample
    p_state, s_state, p_mem = [], [], []
    for l in range(depth):
        lam_init = 0.8 - 0.6 * math.exp(-0.3 * l)
        lw = dict(
            norm_g=norm_g[l], w_gu=w_ffn_gu[l].astype(BF16), w_dn=w_ffn_dn[l].astype(BF16),
            w_in=w_in[l].astype(BF16), w_out=w_out[l].astype(BF16), w_cq=w_cq[l].astype(BF16),
            w_co=w_co[l].astype(BF16), bias=_band_bias(rel_bias[l]),
            consts=jnp.asarray([lam_init], F32), diff_lambda=diff_lambda[l],
            subln_g=subln_g[l].reshape(1, -1))
        mk, mv = _memory_kv(mem_prompt.reshape(b * n_mem, d), mem_norm_g[l].reshape(1, d),
                            w_ckv[l].astype(BF16))
        mk, mv = mk.reshape(b, n_mem, -1), mv.reshape(b, n_mem, -1)
        xp, st_p = _trunk_layer(l, xp, mk, mv, None, lw)
        xs, st_s = _trunk_layer(l, xs, flat(cache_mem_k[l]), flat(cache_mem_v[l]), cache, lw)
        p_state.append(st_p)
        s_state.append(st_s)
        p_mem.append((mk, mv))

    def stack(items, idx, shape):
        return jnp.stack([it[idx] for it in items]).reshape((depth,) + shape)

    keep = min(WINDOW_A, t)
    return (xp, xs,
            stack(p_state, 0, (b, keep, HA, DHA)), stack(p_state, 1, (b, keep, HA, DHA)),
            stack(p_state, 2, (b, t, HB, 2, DHB)), stack(p_state, 3, (b, t, HB, 2 * DHB)),
            stack(p_mem, 0, (b, n_mem, HC, DHC)), stack(p_mem, 1, (b, n_mem, HC, DHC)),
            stack(s_state, 0, (bd, sd, HA, DHA)), stack(s_state, 1, (bd, sd, HA, DHA)),
            stack(s_state, 2, (bd, sd, HB, 2, DHB)), stack(s_state, 3, (bd, sd, HB, 2 * DHB)))
```

```python
import functools
import math

import jax
import jax.numpy as jnp
import numpy as np
from jax import lax
from jax.experimental import pallas as pl
from jax.experimental.pallas import tpu as pltpu

F32 = jnp.float32
BF16 = jnp.bfloat16

CHUNK = 64
BAND_CHUNKS = 8
WINDOW_A = BAND_CHUNKS * CHUNK
BAND = WINDOW_A + CHUNK
HA, DHA = 8, 64
HB, DHB = 4, 64
HC, DHC = 4, 128
REL_CLIP = 128
EPS = 1e-6
NEG_INF = -1e30
LOG2E = math.log2(math.e)
ALIBI_SLOPES = tuple(2.0 ** (-8.0 * (i + 1) / HB) for i in range(HB))

LANES = 128
VMEM_LIMIT = 56 * 1024 * 1024
ROW_TILE = 512
FF_CHUNK = 256
BAND_UNROLL = 2
DIFF_TQ = 256
DIFF_TK = 256
ONES_ROWS = 16
N_SLOPE_TERMS = 4


def _params(n_axes):
    return pltpu.CompilerParams(
        dimension_semantics=("arbitrary",) * n_axes, vmem_limit_bytes=VMEM_LIMIT)


def _rms(x, g):
    ms = jnp.mean(x * x, axis=-1, keepdims=True)
    return x * lax.rsqrt(ms + EPS) * g


def _dot(a, b):
    return jnp.dot(a, b, preferred_element_type=F32)


def _dot_nt(a, b):
    return lax.dot_general(a, b, (((1,), (1,)), ((), ())), preferred_element_type=F32)


def _chunk_of(pos):
    assert CHUNK & (CHUNK - 1) == 0
    return lax.shift_right_logical(pos, jnp.int32(CHUNK.bit_length() - 1))


def _resident(shape):
    zeros = (0,) * len(shape)
    return pl.BlockSpec(shape, lambda *_: zeros, pipeline_mode=pl.Buffered(1))


def _ffn_kernel(x_ref, gpre_ref, gpost_ref, wgu_ref, wdn_ref, o_ref, acc_ref, *, d_ff):
    x = x_ref[...]
    xn = _rms(x, gpre_ref[...]).astype(BF16)
    for c in range(d_ff // FF_CHUNK):
        lo = c * FF_CHUNK
        gate = _dot(xn, wgu_ref[:, lo:lo + FF_CHUNK])
        up = _dot(xn, wgu_ref[:, d_ff + lo:d_ff + lo + FF_CHUNK])
        hidden = (gate * jax.nn.sigmoid(gate) * up).astype(BF16)
        part = _dot(hidden, wdn_ref[lo:lo + FF_CHUNK, :])
        if c == 0:
            acc_ref[...] = part
        else:
            acc_ref[...] += part
    o_ref[...] = x + 0.5 * _rms(acc_ref[...], gpost_ref[...])


def _ffn_half(x, g_pre, g_post, w_gu, w_dn):
    m, d = x.shape
    d_ff = w_dn.shape[0]
    assert d_ff % FF_CHUNK == 0 and m % ROW_TILE == 0
    row = pl.BlockSpec((ROW_TILE, d), lambda i: (i, 0))
    return pl.pallas_call(
        functools.partial(_ffn_kernel, d_ff=d_ff),
        grid=(m // ROW_TILE,),
        in_specs=[row, _resident((1, d)), _resident((1, d)),
                  _resident((d, 2 * d_ff)), _resident((d_ff, d))],
        out_specs=row,
        out_shape=jax.ShapeDtypeStruct((m, d), F32),
        scratch_shapes=[pltpu.VMEM((ROW_TILE, d), F32)],
        compiler_params=_params(1),
        name="ffn_half",
    )(x, g_pre, g_post, w_gu, w_dn)


def _store_head_rows(o_ref, y):
    m, heads = y.shape[0], y.shape[1] // LANES
    for h in range(heads):
        o_ref[0, pl.ds(h, m, stride=heads), :] = y[:, h * LANES:(h + 1) * LANES]


def _proj_kernel(*refs, width, state_layout, n_alias):
    x_ref, g_ref, w_ref, wkt_ref = refs[:4]
    qa_ref, ka_ref, va_ref, qb_ref, kb_ref, vb_ref = refs[4 + n_alias:]
    u = _rms(x_ref[...], g_ref[...]).astype(BF16)
    qa_ref[...] = _dot(u, w_ref[:, 0:width]) * (DHA ** -0.5 * LOG2E)
    ka_ref[...] = _dot(u, w_ref[:, width:2 * width])
    va_ref[...] = _dot(u, w_ref[:, 2 * width:3 * width])
    qb_ref[...] = _dot(u, w_ref[:, 3 * width:4 * width]) * (DHB ** -0.5 * LOG2E)
    vb = _dot(u, w_ref[:, 5 * width:6 * width])
    if state_layout:
        kb_ref[0, 0] = _dot_nt(wkt_ref[...], u)
        _store_head_rows(vb_ref, vb)
    else:
        kb_ref[...] = _dot(u, w_ref[:, 4 * width:5 * width])
        vb_ref[...] = vb


def _in_proj(x, g, w_in, w_kbt, state=None):
    m, d = x.shape
    width = w_in.shape[1] // 6
    row = pl.BlockSpec((ROW_TILE, d), lambda i: (i, 0))
    out = pl.BlockSpec((ROW_TILE, width), lambda i: (i, 0))
    f32_rows = jax.ShapeDtypeStruct((m, width), F32)
    in_specs = [row, _resident((1, d)), _resident(w_in.shape), _resident(w_kbt.shape)]
    args = [x, g, w_in, w_kbt]
    aliases = {}
    if state is None:
        out_specs, out_shape = [out] * 6, [f32_rows] * 6
    else:
        layer, depth, batch, kt_all, v_all = state
        t = m // batch
        per = t // ROW_TILE
        assert t % ROW_TILE == 0
        kt_spec = pl.BlockSpec((1, 1, width, ROW_TILE), lambda i: (layer, i // per, 0, i % per))
        v_spec = pl.BlockSpec((1, ROW_TILE * HB, LANES), lambda i: (layer, i, 0))
        out_specs = [out] * 4 + [kt_spec, v_spec]
        out_shape = [f32_rows] * 4 + [jax.ShapeDtypeStruct((depth, batch, width, t), F32),
                                      jax.ShapeDtypeStruct((depth, m * HB, LANES), F32)]
        if kt_all is not None:
            in_specs += [pl.BlockSpec(memory_space=pl.ANY)] * 2
            args += [kt_all, v_all]
            aliases = {4: 4, 5: 5}
    return pl.pallas_call(
        functools.partial(_proj_kernel, width=width, state_layout=state is not None,
                          n_alias=len(aliases)),
        grid=(m // ROW_TILE,),
        in_specs=in_specs,
        out_specs=out_specs,
        out_shape=out_shape,
        input_output_aliases=aliases,
        compiler_params=_params(1),
        name="in_proj",
    )(*args)


def _memkv_kernel(*refs, width, n_alias):
    x_ref, g_ref, w_ref = refs[:3]
    k_ref, v_ref = refs[3 + n_alias:]
    u = _rms(x_ref[...], g_ref[...]).astype(BF16)
    _store_head_rows(k_ref, _dot(u, w_ref[:, :width]))
    _store_head_rows(v_ref, _dot(u, w_ref[:, width:]))


def _memory_kv(layer, depth, mem, g, w_ckv, prev=None):
    m, d = mem.shape
    width = w_ckv.shape[1] // 2
    row = pl.BlockSpec((ROW_TILE, d), lambda i: (i, 0))
    out = pl.BlockSpec((1, ROW_TILE * HC, LANES), lambda i: (layer, i, 0))
    in_specs = [row, _resident((1, d)), _resident(w_ckv.shape)]
    args = [mem, g, w_ckv]
    aliases = {}
    if prev is not None:
        in_specs += [pl.BlockSpec(memory_space=pl.ANY)] * 2
        args += list(prev)
        aliases = {3: 0, 4: 1}
    return pl.pallas_call(
        functools.partial(_memkv_kernel, width=width, n_alias=len(aliases)),
        grid=(m // ROW_TILE,),
        in_specs=in_specs,
        out_specs=[out] * 2,
        out_shape=[jax.ShapeDtypeStruct((depth, m * HC, LANES), F32)] * 2,
        input_output_aliases=aliases,
        compiler_params=_params(1),
        name="memory_kv",
    )(*args)


def _bias_kernel(table_ref, o_ref):
    pair = pl.program_id(0)
    row = lax.broadcasted_iota(jnp.int32, (2 * CHUNK, BAND), 0)
    j = lax.broadcasted_iota(jnp.int32, (2 * CHUNK, BAND), 1)
    odd = row >= CHUNK
    idx = jnp.clip((row & (CHUNK - 1)) - (j - WINDOW_A), -REL_CLIP, REL_CLIP) + REL_CLIP

    def pick(r, acc):
        return jnp.where(idx == r, jnp.where(odd, table_ref[2 * pair + 1, r], table_ref[2 * pair, r]), acc)

    o_ref[0] = lax.fori_loop(0, 2 * REL_CLIP + 1, pick, jnp.zeros((2 * CHUNK, BAND), F32)) * LOG2E


def _band_bias(table):
    return pl.pallas_call(
        _bias_kernel,
        grid=(HA // 2,),
        in_specs=[pl.BlockSpec(memory_space=pltpu.SMEM)],
        out_specs=pl.BlockSpec((1, 2 * CHUNK, BAND), lambda p: (p, 0, 0)),
        out_shape=jax.ShapeDtypeStruct((HA // 2, 2 * CHUNK, BAND), F32),
        compiler_params=_params(1),
        name="band_bias",
    )(table)


def _split_heads(x):
    lane = lax.broadcasted_iota(jnp.int32, x.shape, 1)
    lo = jnp.where(lane < LANES // 2, x, 0.0).astype(BF16)
    hi = jnp.where(lane >= LANES // 2, x, 0.0).astype(BF16)
    return lo, hi


def _stack_heads(x):
    return jnp.concatenate(_split_heads(x), axis=0)


def _unstack_heads(o):
    m = o.shape[0] // 2
    lane = lax.broadcasted_iota(jnp.int32, (m, LANES), 1)
    return jnp.where(lane < LANES // 2, o[:m], o[m:])


def _exp2_rows(s):
    p = jnp.exp2(s - jnp.max(s, axis=-1, keepdims=True))
    return p.astype(BF16), jnp.sum(p, axis=-1, keepdims=True)


def _band_prompt_kernel(q_ref, kp_ref, kc_ref, vp_ref, vc_ref, bias_ref, o_ref, kcat_ref, vcat_ref):
    t = pl.program_id(1)
    tile = kc_ref.shape[1]
    kcat_ref[0:tile, :] = kp_ref[0].astype(BF16)
    kcat_ref[tile:2 * tile, :] = kc_ref[0].astype(BF16)
    vcat_ref[0:tile, :] = vp_ref[0].astype(BF16)
    vcat_ref[tile:2 * tile, :] = vc_ref[0].astype(BF16)
    col = lax.broadcasted_iota(jnp.int32, (2 * CHUNK, BAND), 1)

    def step(i, carry):
        work = []
        for sub in range(BAND_UNROLL):
            r0 = pl.multiple_of((i * BAND_UNROLL + sub) * CHUNK, CHUNK)
            valid = (col >= WINDOW_A - r0) | (t > 0)
            for pair in range(HA // 2):
                lanes = slice(pair * LANES, (pair + 1) * LANES)
                q2 = _stack_heads(q_ref[0, pl.ds(r0, CHUNK), lanes])
                work.append((r0, pair, lanes, valid, _dot_nt(q2, kcat_ref[pl.ds(r0, BAND), lanes])))
        probs = []
        for r0, pair, lanes, valid, s in work:
            probs.append((r0, lanes) + _exp2_rows(jnp.where(valid, s + bias_ref[pair], NEG_INF)))
        for r0, lanes, p, l in probs:
            o = _unstack_heads(_dot(p, vcat_ref[pl.ds(r0, BAND), lanes]) / l)
            o_ref[0, pl.ds(r0, CHUNK), lanes] = o.astype(o_ref.dtype)
        return carry

    lax.fori_loop(0, tile // (CHUNK * BAND_UNROLL), step, 0)


def _band_prompt(q, k, v, bias):
    b, t, w = q.shape
    tile = WINDOW_A
    assert t % tile == 0
    cur = pl.BlockSpec((1, tile, w), lambda i, j: (i, j, 0))
    prev = pl.BlockSpec((1, tile, w), lambda i, j: (i, jnp.maximum(j - 1, 0), 0))
    return pl.pallas_call(
        _band_prompt_kernel,
        grid=(b, t // tile),
        in_specs=[cur, prev, cur, prev, cur, _resident(bias.shape)],
        out_specs=cur,
        out_shape=jax.ShapeDtypeStruct((b, t, w), BF16),
        scratch_shapes=[pltpu.VMEM((2 * tile, w), BF16), pltpu.VMEM((2 * tile, w), BF16)],
        compiler_params=_params(2),
        name="band_prompt",
    )(q, k, k, v, v, bias)


def _band_sample_kernel(q_ref, kt_ref, kn_ref, vt_ref, vn_ref, bias_ref, o_ref):
    past = kt_ref.shape[3]
    s_len = q_ref.shape[1]
    work = []
    for pair in range(HA // 2):
        lanes = slice(pair * LANES, (pair + 1) * LANES)
        q2 = _stack_heads(q_ref[0, :, lanes])
        s_old = _dot(q2, kt_ref[0, 0, lanes, :].astype(BF16))
        s_new = _dot_nt(q2, kn_ref[0, :, lanes].astype(BF16))
        work.append((pair, lanes, s_old, s_new))
    probs = []
    for pair, lanes, s_old, s_new in work:
        bias = jnp.concatenate([bias_ref[pair, 0:s_len, :], bias_ref[pair, CHUNK:CHUNK + s_len, :]],
                               axis=0)
        s_old = s_old + bias[:, 0:past]
        s_new = s_new + bias[:, past:past + s_len]
        m = jnp.maximum(jnp.max(s_old, axis=-1, keepdims=True), jnp.max(s_new, axis=-1, keepdims=True))
        p_old = jnp.exp2(s_old - m)
        p_new = jnp.exp2(s_new - m)
        l = jnp.sum(p_old, axis=-1, keepdims=True) + jnp.sum(p_new, axis=-1, keepdims=True)
        probs.append((lanes, p_old.astype(BF16), p_new.astype(BF16), l))
    for lanes, p_old, p_new, l in probs:
        o = _dot_nt(p_old, vt_ref[0, 0, lanes, :].astype(BF16)) + _dot(p_new, vn_ref[0, :, lanes].astype(BF16))
        o_ref[0, :, lanes] = _unstack_heads(o / l).astype(o_ref.dtype)


def _band_sample(layer, q, k, v, cache_kt, cache_vt, bias):
    b, s_len, w = q.shape
    past = cache_kt.shape[3]
    assert past == WINDOW_A and s_len <= CHUNK
    new = pl.BlockSpec((1, s_len, w), lambda i: (i, 0, 0))
    old = pl.BlockSpec((1, 1, w, past), lambda i: (layer, i, 0, 0))
    return pl.pallas_call(
        _band_sample_kernel,
        grid=(b,),
        in_specs=[new, old, new, old, new, _resident(bias.shape)],
        out_specs=new,
        out_shape=jax.ShapeDtypeStruct((b, s_len, w), BF16),
        compiler_params=_params(1),
        name="band_sample",
    )(q, cache_kt, k, cache_vt, v, bias)


def _diff_lambda(dl_ref, lam_init):
    lp = dl_ref[...]
    a = jnp.sum(lp[0:1] * lp[1:2], axis=-1, keepdims=True)
    b = jnp.sum(lp[2:3] * lp[3:4], axis=-1, keepdims=True)
    return jnp.exp(a) - jnp.exp(b) + lam_init


def _bf16_terms(x):
    terms, rest = [], float(x)
    for _ in range(N_SLOPE_TERMS):
        term = float(np.asarray(rest, np.float32).astype(BF16))
        terms.append(term)
        rest -= term
    return terms


def _key_position_features(kpos, lane):
    hi = lax.shift_left(_chunk_of(kpos), jnp.int32(CHUNK.bit_length() - 1))
    lo = kpos & (CHUNK - 1)
    feat = jnp.where((lane & 1) == 0, hi, lo)
    return jnp.where(lane < 2 * N_SLOPE_TERMS, feat, 0).astype(F32).astype(BF16)


def _slope_features(head, lane):
    feat = jnp.zeros(lane.shape, F32)
    for i, term in enumerate(_bf16_terms(ALIBI_SLOPES[head] * LOG2E)):
        feat = jnp.where(lax.shift_right_logical(lane, jnp.int32(1)) == i, term, feat)
    return feat.astype(BF16)


def _diff_prompt_kernel(consts_ref, dl_ref, sg_ref, q_ref, kt_ref, v_ref, o_ref,
                        kaug_ref, vt_ref, acc_ref):
    qi = pl.program_id(1)
    tq, tk = DIFF_TQ, DIFF_TK
    t = kt_ref.shape[3]
    lam_init = consts_ref[0]
    lam = _diff_lambda(dl_ref, lam_init)

    @pl.when(qi == 0)
    def _():
        lane = lax.broadcasted_iota(jnp.int32, (tk, LANES), 1)
        row = lax.broadcasted_iota(jnp.int32, (tk, LANES), 0)

        def build(blk, carry):
            r0 = pl.multiple_of(blk * tk, tk)
            feat = _key_position_features(r0 + row, lane)
            for h in range(HB):
                k = kt_ref[0, 0, h * LANES:(h + 1) * LANES, pl.ds(r0, tk)]
                v = v_ref[0, pl.ds(r0 * HB + h, tk, stride=HB), :]
                kaug_ref[h, pl.ds(r0, tk), 0:LANES] = k.T.astype(BF16)
                kaug_ref[h, pl.ds(r0, tk), LANES:2 * LANES] = feat
                vt_ref[h, blk, 0:LANES, :] = v.T.astype(BF16)
                vt_ref[h, blk, LANES:LANES + ONES_ROWS, :] = jnp.ones((ONES_ROWS, tk), BF16)
            return carry

        lax.fori_loop(0, t // tk, build, 0)

    q = q_ref[0]
    qlane = lax.broadcasted_iota(jnp.int32, (tq, LANES), 1)
    qd = []
    for h in range(HB):
        q_lo, q_hi = _split_heads(q[:, h * LANES:(h + 1) * LANES])
        feat = _slope_features(h, qlane)
        qd.append(jnp.concatenate([jnp.concatenate([q_lo, feat], axis=1),
                                   jnp.concatenate([q_hi, feat], axis=1)], axis=0))
        acc_ref[h] = jnp.zeros(acc_ref.shape[1:], F32)

    def scores(j):
        off = pl.multiple_of(j * tk, tk)
        return [_dot_nt(kaug_ref[h, pl.ds(off, tk), :], qd[h]) for h in range(HB)]

    def update(j, ss, ms):
        out = []
        for h in range(HB):
            m_new = jnp.maximum(ms[h], jnp.max(ss[h], axis=0, keepdims=True))
            p = jnp.exp2(ss[h] - m_new).astype(BF16)
            acc_ref[h] = jnp.exp2(ms[h] - m_new) * acc_ref[h] + _dot(vt_ref[h, j], p)
            out.append(m_new)
        return tuple(out)

    def one_block(j, ms):
        return update(j, scores(j), ms)

    def two_blocks(jj, ms):
        s0, s1 = scores(2 * jj), scores(2 * jj + 1)
        return update(2 * jj + 1, s1, update(2 * jj, s0, ms))

    jd = (qi * tq) // tk
    ms = tuple(jnp.full((1, 2 * tq), NEG_INF, F32) for _ in range(HB))
    ms = lax.fori_loop(0, jd // 2, two_blocks, ms)
    ms = lax.cond(jd % 2 == 1, lambda c: one_block(jd - 1, c), lambda c: c, ms)

    kpos = jd * tk + lax.broadcasted_iota(jnp.int32, (tk, 2 * tq), 0)
    qpos = qi * tq + (lax.broadcasted_iota(jnp.int32, (tk, 2 * tq), 1) & (tq - 1))
    ahead = jnp.maximum(kpos - qpos, 0).astype(F32)
    visible = _chunk_of(kpos) <= _chunk_of(qpos)
    ss = [jnp.where(visible, s - (2.0 * ALIBI_SLOPES[h] * LOG2E) * ahead, NEG_INF)
          for h, s in enumerate(scores(jd))]
    update(jd, ss, ms)
    for h in range(HB):
        acc = acc_ref[h, 0:LANES, :] / acc_ref[h, LANES:LANES + 1, :]
        o = (acc[:, :tq] - lam * acc[:, tq:]).T
        o = _rms(o, sg_ref[...]) * (1.0 - lam_init)
        o_ref[0, :, h * LANES:(h + 1) * LANES] = o.astype(o_ref.dtype)


def _diff_prompt(layer, q, kt_all, v_all, consts, diff_lambda, subln_g):
    b, t, w = q.shape
    tq, tk = DIFF_TQ, DIFF_TK
    assert t % tk == 0 and tk == tq and tq % CHUNK == 0 and t < 2 ** 14
    assert tq & (tq - 1) == 0
    smem = pl.BlockSpec(memory_space=pltpu.SMEM)
    qspec = pl.BlockSpec((1, tq, w), lambda i, j: (i, j, 0))
    kspec = pl.BlockSpec((1, 1, w, t), lambda i, j: (layer, i, 0, 0), pipeline_mode=pl.Buffered(1))
    vspec = pl.BlockSpec((1, t * HB, LANES), lambda i, j: (layer, i, 0), pipeline_mode=pl.Buffered(1))
    return pl.pallas_call(
        _diff_prompt_kernel,
        grid=(b, t // tq),
        in_specs=[smem, _resident(diff_lambda.shape), _resident(subln_g.shape), qspec, kspec, vspec],
        out_specs=qspec,
        out_shape=jax.ShapeDtypeStruct((b, t, w), BF16),
        scratch_shapes=[pltpu.VMEM((HB, t, 2 * LANES), BF16),
                        pltpu.VMEM((HB, t // tk, LANES + ONES_ROWS, tk), BF16),
                        pltpu.VMEM((HB, LANES + ONES_ROWS, 2 * tq), F32)],
        compiler_params=_params(2),
        name="diff_prompt",
    )(consts, diff_lambda, subln_g, q, kt_all, v_all)


def _diff_sample_kernel(consts_ref, dl_ref, sg_ref, q_ref, kt_ref, kn_ref, v_ref, vn_ref, o_ref):
    past = kt_ref.shape[4]
    s_len = q_ref.shape[1]
    lam_init = consts_ref[0]
    lam = _diff_lambda(dl_ref, lam_init)
    kpos = lax.broadcasted_iota(jnp.int32, (1, past), 1).astype(F32)
    row = lax.broadcasted_iota(jnp.int32, (2 * s_len, 1), 0)
    qloc = jnp.where(row >= s_len, row - s_len, row)
    kcol = lax.broadcasted_iota(jnp.int32, (1, s_len), 1)
    near = (past + qloc - jnp.abs(qloc - kcol)).astype(F32)
    for h in range(HB):
        lanes = slice(h * LANES, (h + 1) * LANES)
        c = ALIBI_SLOPES[h] * LOG2E
        q_lo, q_hi = _split_heads(q_ref[0, :, lanes])
        q2 = jnp.concatenate([q_lo, q_hi], axis=0)
        s_old = _dot(q2, kt_ref[0, 0, h].astype(BF16)) + c * kpos
        s_new = _dot_nt(q2, kn_ref[0, :, lanes].astype(BF16)) + c * near
        m = jnp.maximum(jnp.max(s_old, axis=-1, keepdims=True),
                        jnp.max(s_new, axis=-1, keepdims=True))
        p_old = jnp.exp2(s_old - m)
        p_new = jnp.exp2(s_new - m)
        l = jnp.sum(p_old, axis=-1, keepdims=True) + jnp.sum(p_new, axis=-1, keepdims=True)
        v_old = v_ref[0, 0, pl.ds(h, past, stride=HB), :].astype(BF16)
        acc = _dot(p_old.astype(BF16), v_old) + _dot(p_new.astype(BF16), vn_ref[0, :, lanes].astype(BF16))
        o = acc / l
        o = o[:s_len] - lam * o[s_len:]
        o = _rms(o, sg_ref[...]) * (1.0 - lam_init)
        o_ref[0, :, lanes] = o.astype(o_ref.dtype)


def _diff_sample(layer, q, k, v, cache_kt, cache_v, consts, diff_lambda, subln_g):
    b, s_len, w = q.shape
    past = cache_kt.shape[4]
    assert past % CHUNK + s_len <= CHUNK
    smem = pl.BlockSpec(memory_space=pltpu.SMEM)
    new = pl.BlockSpec((1, s_len, w), lambda i: (i, 0, 0))
    old_k = pl.BlockSpec((1, 1, HB, LANES, past), lambda i: (layer, i, 0, 0, 0))
    old_v = pl.BlockSpec((1, 1, past * HB, LANES), lambda i: (layer, i, 0, 0))
    return pl.pallas_call(
        _diff_sample_kernel,
        grid=(b,),
        in_specs=[smem, _resident(diff_lambda.shape), _resident(subln_g.shape),
                  new, old_k, new, old_v, new],
        out_specs=new,
        out_shape=jax.ShapeDtypeStruct((b, s_len, w), BF16),
        compiler_params=_params(1),
        name="diff_sample",
    )(consts, diff_lambda, subln_g, q, cache_kt, k, cache_v, v)


def _softmax_pv(s, v):
    m = jnp.max(s, axis=-1, keepdims=True)
    p = jnp.exp(s - m)
    l = jnp.sum(p, axis=-1, keepdims=True)
    return _dot(p.astype(BF16), v) / l


def _mix_kernel(x_ref, oa_ref, ob_ref, mk_ref, mv_ref, g_ref, wout_ref, wcq_ref, wco_ref, o_ref):
    x = x_ref[0]
    wa = oa_ref.shape[2]
    n_mem = mk_ref.shape[1] // HC
    mix = _dot(oa_ref[0], wout_ref[:wa, :]) + _dot(ob_ref[0], wout_ref[wa:, :])
    x = x + _rms(mix, g_ref[0:1])
    qc = _dot(_rms(x, g_ref[1:2]).astype(BF16), wcq_ref[...]).astype(BF16)
    heads = []
    for h in range(HC):
        lanes = slice(h * DHC, (h + 1) * DHC)
        mk = mk_ref[0, pl.ds(h, n_mem, stride=HC), :].astype(BF16)
        mv = mv_ref[0, pl.ds(h, n_mem, stride=HC), :].astype(BF16)
        s = _dot_nt(qc[:, lanes], mk) * DHC ** -0.5
        heads.append(_softmax_pv(s, mv).astype(BF16))
    oc = _dot(jnp.concatenate(heads, axis=-1), wco_ref[...])
    o_ref[0] = x + _rms(oc, g_ref[2:3])


def _mix_cross(layer, x, oa, ob, mem_k, mem_v, g345, w_out, w_cq, w_co):
    b, t, d = x.shape
    tm = min(ROW_TILE, t)
    assert t % tm == 0
    rows_mem = mem_k.shape[1] // b

    def rows(width):
        return pl.BlockSpec((1, tm, width), lambda i, j: (i, j, 0))

    mem = pl.BlockSpec((1, rows_mem, LANES), lambda i, j: (layer, i, 0))
    return pl.pallas_call(
        _mix_kernel,
        grid=(b, t // tm),
        in_specs=[rows(d), rows(oa.shape[2]), rows(ob.shape[2]), mem, mem, _resident(g345.shape),
                  _resident(w_out.shape), _resident(w_cq.shape), _resident(w_co.shape)],
        out_specs=rows(d),
        out_shape=jax.ShapeDtypeStruct((b, t, d), F32),
        compiler_params=_params(2),
        name="mix_cross",
    )(x, oa, ob, mem_k, mem_v, g345, w_out, w_cq, w_co)


def _trunk_layer(layer, depth, x, mem_k, mem_v, cache, lw, carried=None):
    b, t, d = x.shape
    g = lw["norm_g"]
    x2 = _ffn_half(x.reshape(b * t, d), g[0:1], g[1:2], lw["w_gu"][0], lw["w_dn"][0])
    if cache is None:
        kt_prev, v_prev = carried if carried is not None else (None, None)
        qa, ka, va, qb, kt_all, v_all = _in_proj(x2, g[2:3], lw["w_in"], lw["w_kbt"],
                                                 (layer, depth, b, kt_prev, v_prev))
        qa, ka, va, qb = (y.reshape(b, t, -1) for y in (qa, ka, va, qb))
        oa = _band_prompt(qa, ka, va, lw["bias"])
        ob = _diff_prompt(layer, qb, kt_all, v_all, lw["consts"], lw["diff_lambda"], lw["subln_g"])
        keep = min(WINDOW_A, t)
        state = (ka[:, t - keep:], va[:, t - keep:], kt_all, v_all)
    else:
        qa, ka, va, qb, kb, vb = (y.reshape(b, t, -1)
                                  for y in _in_proj(x2, g[2:3], lw["w_in"], lw["w_kbt"]))
        ca_kt, ca_vt, cb_kt, cb_v = cache
        oa = _band_sample(layer, qa, ka, va, ca_kt, ca_vt, lw["bias"])
        ob = _diff_sample(layer, qb, kb, vb, cb_kt, cb_v, lw["consts"], lw["diff_lambda"],
                          lw["subln_g"])
        state = (ka, va, kb, vb)
    x3 = _mix_cross(layer, x2.reshape(b, t, d), oa, ob, mem_k, mem_v, g[3:6], lw["w_out"],
                    lw["w_cq"], lw["w_co"])
    x4 = _ffn_half(x3.reshape(b * t, d), g[6:7], g[7:8], lw["w_gu"][1], lw["w_dn"][1])
    return x4.reshape(b, t, d), state


def kernel(x_prompt, x_sample, mem_prompt, cache_a_k, cache_a_v, cache_b_k, cache_b_v, cache_mem_k,
           cache_mem_v, norm_g, w_ffn_gu, w_ffn_dn, w_in, rel_bias, diff_lambda, subln_g, w_out,
           mem_norm_g, w_cq, w_ckv, w_co):
    depth = norm_g.shape[0]
    b, t, d = x_prompt.shape
    bd, sd, _ = x_sample.shape
    n_mem = mem_prompt.shape[1]
    past_a, past_b = cache_a_k.shape[2], cache_b_k.shape[2]
    width = w_in.shape[2] // 6

    cache = (jnp.transpose(cache_a_k, (0, 1, 3, 4, 2)).reshape(depth, bd, HA * DHA, past_a),
             jnp.transpose(cache_a_v, (0, 1, 3, 4, 2)).reshape(depth, bd, HA * DHA, past_a),
             jnp.transpose(cache_b_k, (0, 1, 3, 4, 5, 2)).reshape(depth, bd, HB, 2 * DHB, past_b),
             cache_b_v.reshape(depth, bd, past_b * HB, 2 * DHB))
    smem_k = cache_mem_k.reshape(depth, bd * n_mem * HC, DHC)
    smem_v = cache_mem_v.reshape(depth, bd * n_mem * HC, DHC)

    xp, xs = x_prompt, x_sample
    p_a, s_state = [], []
    pmem = carried = None
    for l in range(depth):
        lam_init = 0.8 - 0.6 * math.exp(-0.3 * l)
        lw = dict(
            norm_g=norm_g[l], w_gu=w_ffn_gu[l].astype(BF16), w_dn=w_ffn_dn[l].astype(BF16),
            w_in=w_in[l].astype(BF16), w_kbt=w_in[l][:, 4 * width:5 * width].T.astype(BF16),
            w_out=w_out[l].astype(BF16), w_cq=w_cq[l].astype(BF16), w_co=w_co[l].astype(BF16),
            bias=_band_bias(rel_bias[l]), consts=jnp.asarray([lam_init], F32),
            diff_lambda=diff_lambda[l], subln_g=subln_g[l].reshape(1, -1))
        pmem = _memory_kv(l, depth, mem_prompt.reshape(b * n_mem, d), mem_norm_g[l].reshape(1, d),
                          w_ckv[l].astype(BF16), pmem)
        xp, st_p = _trunk_layer(l, depth, xp, pmem[0], pmem[1], None, lw, carried)
        xs, st_s = _trunk_layer(l, depth, xs, smem_k, smem_v, cache, lw)
        carried = st_p[2:]
        p_a.append(st_p[:2])
        s_state.append(st_s)

    def stack(items, idx, shape):
        return jnp.stack([it[idx] for it in items]).reshape((depth,) + shape)

    keep = min(WINDOW_A, t)
    kt_all, v_all = carried
    return (xp, xs,
            stack(p_a, 0, (b, keep, HA, DHA)), stack(p_a, 1, (b, keep, HA, DHA)),
            jnp.transpose(kt_all.reshape(depth, b, HB, 2, DHB, t), (0, 1, 5, 2, 3, 4)),
            v_all.reshape(depth, b, t, HB, 2 * DHB),
            pmem[0].reshape(depth, b, n_mem, HC, DHC), pmem[1].reshape(depth, b, n_mem, HC, DHC),
            stack(s_state, 0, (bd, sd, HA, DHA)), stack(s_state, 1, (bd, sd, HA, DHA)),
            stack(s_state, 2, (bd, sd, HB, 2, DHB)), stack(s_state, 3, (bd, sd, HB, 2 * DHB)))
```

```python
import functools
import math

import jax
import jax.numpy as jnp
import numpy as np
from jax import lax
from jax.experimental import pallas as pl
from jax.experimental.pallas import tpu as pltpu

F32 = jnp.float32
BF16 = jnp.bfloat16

CHUNK = 64
BAND_CHUNKS = 8
WINDOW_A = BAND_CHUNKS * CHUNK
BAND = WINDOW_A + CHUNK
HA, DHA = 8, 64
HB, DHB = 4, 64
HC, DHC = 4, 128
REL_CLIP = 128
EPS = 1e-6
NEG_INF = -1e30
LOG2E = math.log2(math.e)
ALIBI_SLOPES = tuple(2.0 ** (-8.0 * (i + 1) / HB) for i in range(HB))

LANES = 128
VMEM_LIMIT = 56 * 1024 * 1024
ROW_TILE = 512
FF_CHUNK = 256
MIX_BATCHES = 8
BAND_UNROLL = 2
DIFF_TQ = 256
DIFF_TK = 256
ONES_ROWS = 16
N_SLOPE_TERMS = 4


def _params(n_axes):
    return pltpu.CompilerParams(
        dimension_semantics=("arbitrary",) * n_axes, vmem_limit_bytes=VMEM_LIMIT)


def _rms(x, g):
    ms = jnp.mean(x * x, axis=-1, keepdims=True)
    return x * lax.rsqrt(ms + EPS) * g


def _dot(a, b):
    return jnp.dot(a, b, preferred_element_type=F32)


def _dot_nt(a, b):
    return lax.dot_general(a, b, (((1,), (1,)), ((), ())), preferred_element_type=F32)


def _chunk_of(pos):
    assert CHUNK & (CHUNK - 1) == 0
    return lax.shift_right_logical(pos, jnp.int32(CHUNK.bit_length() - 1))


def _resident(shape):
    zeros = (0,) * len(shape)
    return pl.BlockSpec(shape, lambda *_: zeros, pipeline_mode=pl.Buffered(1))


def _weight(w, lead):
    index = tuple(lead) + (0,) * (w.ndim - len(lead))
    return pl.BlockSpec((None,) * len(lead) + w.shape[len(lead):], lambda *_: index,
                        pipeline_mode=pl.Buffered(1))


def _ffn_kernel(x_ref, gpre_ref, gpost_ref, wgu_ref, wdn_ref, o_ref, acc_ref, *, d_ff):
    x = x_ref[...]
    xn = _rms(x, gpre_ref[...]).astype(BF16)
    for c in range(d_ff // FF_CHUNK):
        lo = c * FF_CHUNK
        gate = _dot(xn, wgu_ref[:, lo:lo + FF_CHUNK])
        up = _dot(xn, wgu_ref[:, d_ff + lo:d_ff + lo + FF_CHUNK])
        hidden = (gate * jax.nn.sigmoid(gate) * up).astype(BF16)
        part = _dot(hidden, wdn_ref[lo:lo + FF_CHUNK, :])
        if c == 0:
            acc_ref[...] = part
        else:
            acc_ref[...] += part
    o_ref[...] = x + 0.5 * _rms(acc_ref[...], gpost_ref[...])


def _ffn_half(x, g_pre, g_post, w_gu, w_dn, lead):
    m, d = x.shape
    d_ff = w_dn.shape[-2]
    assert d_ff % FF_CHUNK == 0 and m % ROW_TILE == 0
    row = pl.BlockSpec((ROW_TILE, d), lambda i: (i, 0))
    return pl.pallas_call(
        functools.partial(_ffn_kernel, d_ff=d_ff),
        grid=(m // ROW_TILE,),
        in_specs=[row, _resident((1, d)), _resident((1, d)), _weight(w_gu, lead), _weight(w_dn, lead)],
        out_specs=row,
        out_shape=jax.ShapeDtypeStruct((m, d), F32),
        scratch_shapes=[pltpu.VMEM((ROW_TILE, d), F32)],
        compiler_params=_params(1),
        name="ffn_half",
    )(x, g_pre, g_post, w_gu, w_dn)


def _store_head_rows(o_ref, y):
    m, heads = y.shape[0], y.shape[1] // LANES
    for h in range(heads):
        o_ref[0, pl.ds(h, m, stride=heads), :] = y[:, h * LANES:(h + 1) * LANES]


def _proj_kernel(*refs, width, state_layout, n_alias):
    x_ref, g_ref, w_ref, wkt_ref = refs[:4]
    qa_ref, ka_ref, va_ref, qb_ref, kb_ref, vb_ref = refs[4 + n_alias:]
    u = _rms(x_ref[...], g_ref[...]).astype(BF16)
    qa_ref[...] = _dot(u, w_ref[:, 0:width]) * (DHA ** -0.5 * LOG2E)
    ka_ref[...] = _dot(u, w_ref[:, width:2 * width])
    va_ref[...] = _dot(u, w_ref[:, 2 * width:3 * width])
    qb_ref[...] = _dot(u, w_ref[:, 3 * width:4 * width]) * (DHB ** -0.5 * LOG2E)
    vb = _dot(u, w_ref[:, 5 * width:6 * width])
    if state_layout:
        kb_ref[0, 0] = _dot_nt(wkt_ref[...], u)
        _store_head_rows(vb_ref, vb)
    else:
        kb_ref[...] = _dot(u, w_ref[:, 4 * width:5 * width])
        vb_ref[...] = vb


def _in_proj(layer, x, g, w_in, w_kbt, state=None):
    m, d = x.shape
    width = w_in.shape[-1] // 6
    row = pl.BlockSpec((ROW_TILE, d), lambda i: (i, 0))
    out = pl.BlockSpec((ROW_TILE, width), lambda i: (i, 0))
    f32_rows = jax.ShapeDtypeStruct((m, width), F32)
    in_specs = [row, _resident((1, d)), _weight(w_in, (layer,)), _weight(w_kbt, (layer,))]
    args = [x, g, w_in, w_kbt]
    aliases = {}
    if state is None:
        out_specs, out_shape = [out] * 6, [f32_rows] * 6
    else:
        depth, batch, kt_all, v_all = state
        t = m // batch
        per = t // ROW_TILE
        assert t % ROW_TILE == 0
        kt_spec = pl.BlockSpec((1, 1, width, ROW_TILE), lambda i: (layer, i // per, 0, i % per))
        v_spec = pl.BlockSpec((1, ROW_TILE * HB, LANES), lambda i: (layer, i, 0))
        out_specs = [out] * 4 + [kt_spec, v_spec]
        out_shape = [f32_rows] * 4 + [jax.ShapeDtypeStruct((depth, batch, width, t), F32),
                                      jax.ShapeDtypeStruct((depth, m * HB, LANES), F32)]
        if kt_all is not None:
            in_specs += [pl.BlockSpec(memory_space=pl.ANY)] * 2
            args += [kt_all, v_all]
            aliases = {4: 4, 5: 5}
    return pl.pallas_call(
        functools.partial(_proj_kernel, width=width, state_layout=state is not None,
                          n_alias=len(aliases)),
        grid=(m // ROW_TILE,),
        in_specs=in_specs,
        out_specs=out_specs,
        out_shape=out_shape,
        input_output_aliases=aliases,
        compiler_params=_params(1),
        name="in_proj",
    )(*args)


def _memkv_kernel(*refs, width, n_alias):
    x_ref, g_ref, w_ref = refs[:3]
    k_ref, v_ref = refs[3 + n_alias:]
    u = _rms(x_ref[...], g_ref[...]).astype(BF16)
    _store_head_rows(k_ref, _dot(u, w_ref[:, :width]))
    _store_head_rows(v_ref, _dot(u, w_ref[:, width:]))


def _memory_kv(layer, depth, mem, g, w_ckv, prev=None):
    m, d = mem.shape
    width = w_ckv.shape[-1] // 2
    row = pl.BlockSpec((ROW_TILE, d), lambda i: (i, 0))
    out = pl.BlockSpec((1, ROW_TILE * HC, LANES), lambda i: (layer, i, 0))
    in_specs = [row, _resident((1, d)), _weight(w_ckv, (layer,))]
    args = [mem, g, w_ckv]
    aliases = {}
    if prev is not None:
        in_specs += [pl.BlockSpec(memory_space=pl.ANY)] * 2
        args += list(prev)
        aliases = {3: 0, 4: 1}
    return pl.pallas_call(
        functools.partial(_memkv_kernel, width=width, n_alias=len(aliases)),
        grid=(m // ROW_TILE,),
        in_specs=in_specs,
        out_specs=[out] * 2,
        out_shape=[jax.ShapeDtypeStruct((depth, m * HC, LANES), F32)] * 2,
        input_output_aliases=aliases,
        compiler_params=_params(1),
        name="memory_kv",
    )(*args)


def _bias_kernel(table_ref, o_ref):
    pair = pl.program_id(0)
    row = lax.broadcasted_iota(jnp.int32, (2 * CHUNK, BAND), 0)
    j = lax.broadcasted_iota(jnp.int32, (2 * CHUNK, BAND), 1)
    odd = row >= CHUNK
    idx = jnp.clip((row & (CHUNK - 1)) - (j - WINDOW_A), -REL_CLIP, REL_CLIP) + REL_CLIP

    def pick(r, acc):
        return jnp.where(idx == r, jnp.where(odd, table_ref[2 * pair + 1, r], table_ref[2 * pair, r]), acc)

    o_ref[0] = lax.fori_loop(0, 2 * REL_CLIP + 1, pick, jnp.zeros((2 * CHUNK, BAND), F32)) * LOG2E


def _band_bias(table):
    return pl.pallas_call(
        _bias_kernel,
        grid=(HA // 2,),
        in_specs=[pl.BlockSpec(memory_space=pltpu.SMEM)],
        out_specs=pl.BlockSpec((1, 2 * CHUNK, BAND), lambda p: (p, 0, 0)),
        out_shape=jax.ShapeDtypeStruct((HA // 2, 2 * CHUNK, BAND), F32),
        compiler_params=_params(1),
        name="band_bias",
    )(table)


def _split_heads(x):
    lane = lax.broadcasted_iota(jnp.int32, x.shape, 1)
    lo = jnp.where(lane < LANES // 2, x, 0.0).astype(BF16)
    hi = jnp.where(lane >= LANES // 2, x, 0.0).astype(BF16)
    return lo, hi


def _stack_heads(x):
    return jnp.concatenate(_split_heads(x), axis=0)


def _unstack_heads(o):
    m = o.shape[0] // 2
    lane = lax.broadcasted_iota(jnp.int32, (m, LANES), 1)
    return jnp.where(lane < LANES // 2, o[:m], o[m:])


def _exp2_rows(s):
    p = jnp.exp2(s - jnp.max(s, axis=-1, keepdims=True))
    return p.astype(BF16), jnp.sum(p, axis=-1, keepdims=True)


def _band_prompt_kernel(q_ref, kp_ref, kc_ref, vp_ref, vc_ref, bias_ref, o_ref, kcat_ref, vcat_ref):
    t = pl.program_id(1)
    tile = kc_ref.shape[1]
    kcat_ref[0:tile, :] = kp_ref[0].astype(BF16)
    kcat_ref[tile:2 * tile, :] = kc_ref[0].astype(BF16)
    vcat_ref[0:tile, :] = vp_ref[0].astype(BF16)
    vcat_ref[tile:2 * tile, :] = vc_ref[0].astype(BF16)
    col = lax.broadcasted_iota(jnp.int32, (2 * CHUNK, BAND), 1)

    def step(i, carry):
        work = []
        for sub in range(BAND_UNROLL):
            r0 = pl.multiple_of((i * BAND_UNROLL + sub) * CHUNK, CHUNK)
            valid = (col >= WINDOW_A - r0) | (t > 0)
            for pair in range(HA // 2):
                lanes = slice(pair * LANES, (pair + 1) * LANES)
                q2 = _stack_heads(q_ref[0, pl.ds(r0, CHUNK), lanes])
                work.append((r0, pair, lanes, valid, _dot_nt(q2, kcat_ref[pl.ds(r0, BAND), lanes])))
        probs = []
        for r0, pair, lanes, valid, s in work:
            probs.append((r0, lanes) + _exp2_rows(jnp.where(valid, s + bias_ref[pair], NEG_INF)))
        for r0, lanes, p, l in probs:
            o = _unstack_heads(_dot(p, vcat_ref[pl.ds(r0, BAND), lanes]) / l)
            o_ref[0, pl.ds(r0, CHUNK), lanes] = o.astype(o_ref.dtype)
        return carry

    lax.fori_loop(0, tile // (CHUNK * BAND_UNROLL), step, 0)


def _band_prompt(q, k, v, bias):
    b, t, w = q.shape
    tile = WINDOW_A
    assert t % tile == 0
    cur = pl.BlockSpec((1, tile, w), lambda i, j: (i, j, 0))
    prev = pl.BlockSpec((1, tile, w), lambda i, j: (i, jnp.maximum(j - 1, 0), 0))
    return pl.pallas_call(
        _band_prompt_kernel,
        grid=(b, t // tile),
        in_specs=[cur, prev, cur, prev, cur, _resident(bias.shape)],
        out_specs=cur,
        out_shape=jax.ShapeDtypeStruct((b, t, w), BF16),
        scratch_shapes=[pltpu.VMEM((2 * tile, w), BF16), pltpu.VMEM((2 * tile, w), BF16)],
        compiler_params=_params(2),
        name="band_prompt",
    )(q, k, k, v, v, bias)


def _band_sample_kernel(q_ref, kt_ref, kn_ref, vt_ref, vn_ref, bias_ref, o_ref):
    past = kt_ref.shape[3]
    s_len = q_ref.shape[1]
    work = []
    for pair in range(HA // 2):
        lanes = slice(pair * LANES, (pair + 1) * LANES)
        q2 = _stack_heads(q_ref[0, :, lanes])
        s_old = _dot(q2, kt_ref[0, 0, lanes, :].astype(BF16))
        s_new = _dot_nt(q2, kn_ref[0, :, lanes].astype(BF16))
        work.append((pair, lanes, s_old, s_new))
    probs = []
    for pair, lanes, s_old, s_new in work:
        bias = jnp.concatenate([bias_ref[pair, 0:s_len, :], bias_ref[pair, CHUNK:CHUNK + s_len, :]],
                               axis=0)
        s_old = s_old + bias[:, 0:past]
        s_new = s_new + bias[:, past:past + s_len]
        m = jnp.maximum(jnp.max(s_old, axis=-1, keepdims=True), jnp.max(s_new, axis=-1, keepdims=True))
        p_old = jnp.exp2(s_old - m)
        p_new = jnp.exp2(s_new - m)
        l = jnp.sum(p_old, axis=-1, keepdims=True) + jnp.sum(p_new, axis=-1, keepdims=True)
        probs.append((lanes, p_old.astype(BF16), p_new.astype(BF16), l))
    for lanes, p_old, p_new, l in probs:
        o = _dot_nt(p_old, vt_ref[0, 0, lanes, :].astype(BF16)) + _dot(p_new, vn_ref[0, :, lanes].astype(BF16))
        o_ref[0, :, lanes] = _unstack_heads(o / l).astype(o_ref.dtype)


def _band_sample(layer, q, k, v, cache_kt, cache_vt, bias):
    b, s_len, w = q.shape
    past = cache_kt.shape[3]
    assert past == WINDOW_A and s_len <= CHUNK
    new = pl.BlockSpec((1, s_len, w), lambda i: (i, 0, 0))
    old = pl.BlockSpec((1, 1, w, past), lambda i: (layer, i, 0, 0))
    return pl.pallas_call(
        _band_sample_kernel,
        grid=(b,),
        in_specs=[new, old, new, old, new, _resident(bias.shape)],
        out_specs=new,
        out_shape=jax.ShapeDtypeStruct((b, s_len, w), BF16),
        compiler_params=_params(1),
        name="band_sample",
    )(q, cache_kt, k, cache_vt, v, bias)


def _diff_lambda(dl_ref, lam_init):
    lp = dl_ref[...]
    a = jnp.sum(lp[0:1] * lp[1:2], axis=-1, keepdims=True)
    b = jnp.sum(lp[2:3] * lp[3:4], axis=-1, keepdims=True)
    return jnp.exp(a) - jnp.exp(b) + lam_init


def _bf16_terms(x):
    terms, rest = [], float(x)
    for _ in range(N_SLOPE_TERMS):
        term = float(np.asarray(rest, np.float32).astype(BF16))
        terms.append(term)
        rest -= term
    return terms


def _key_position_features(kpos, lane):
    hi = lax.shift_left(_chunk_of(kpos), jnp.int32(CHUNK.bit_length() - 1))
    lo = kpos & (CHUNK - 1)
    feat = jnp.where((lane & 1) == 0, hi, lo)
    return jnp.where(lane < 2 * N_SLOPE_TERMS, feat, 0).astype(F32).astype(BF16)


def _slope_features(head, lane):
    feat = jnp.zeros(lane.shape, F32)
    for i, term in enumerate(_bf16_terms(ALIBI_SLOPES[head] * LOG2E)):
        feat = jnp.where(lax.shift_right_logical(lane, jnp.int32(1)) == i, term, feat)
    return feat.astype(BF16)


def _diff_prompt_kernel(consts_ref, dl_ref, sg_ref, q_ref, kt_ref, v_ref, o_ref,
                        kaug_ref, vt_ref, acc_ref):
    qi = pl.program_id(1)
    tq, tk = DIFF_TQ, DIFF_TK
    t = kt_ref.shape[3]
    lam_init = consts_ref[0]
    lam = _diff_lambda(dl_ref, lam_init)

    @pl.when(qi == 0)
    def _():
        lane = lax.broadcasted_iota(jnp.int32, (tk, LANES), 1)
        row = lax.broadcasted_iota(jnp.int32, (tk, LANES), 0)

        def build(blk, carry):
            r0 = pl.multiple_of(blk * tk, tk)
            feat = _key_position_features(r0 + row, lane)
            for h in range(HB):
                k = kt_ref[0, 0, h * LANES:(h + 1) * LANES, pl.ds(r0, tk)]
                v = v_ref[0, pl.ds(r0 * HB + h, tk, stride=HB), :]
                kaug_ref[h, pl.ds(r0, tk), 0:LANES] = k.T.astype(BF16)
                kaug_ref[h, pl.ds(r0, tk), LANES:2 * LANES] = feat
                vt_ref[h, blk, 0:LANES, :] = v.T.astype(BF16)
                vt_ref[h, blk, LANES:LANES + ONES_ROWS, :] = jnp.ones((ONES_ROWS, tk), BF16)
            return carry

        lax.fori_loop(0, t // tk, build, 0)

    q = q_ref[0]
    qlane = lax.broadcasted_iota(jnp.int32, (tq, LANES), 1)
    qd = []
    for h in range(HB):
        q_lo, q_hi = _split_heads(q[:, h * LANES:(h + 1) * LANES])
        feat = _slope_features(h, qlane)
        qd.append(jnp.concatenate([jnp.concatenate([q_lo, feat], axis=1),
                                   jnp.concatenate([q_hi, feat], axis=1)], axis=0))
        acc_ref[h] = jnp.zeros(acc_ref.shape[1:], F32)

    def scores(j):
        off = pl.multiple_of(j * tk, tk)
        return [_dot_nt(kaug_ref[h, pl.ds(off, tk), :], qd[h]) for h in range(HB)]

    def update(j, ss, ms):
        out = []
        for h in range(HB):
            m_new = jnp.maximum(ms[h], jnp.max(ss[h], axis=0, keepdims=True))
            p = jnp.exp2(ss[h] - m_new).astype(BF16)
            acc_ref[h] = jnp.exp2(ms[h] - m_new) * acc_ref[h] + _dot(vt_ref[h, j], p)
            out.append(m_new)
        return tuple(out)

    def blocks(n):
        def body(jj, ms):
            ss = [scores(n * jj + i) for i in range(n)]
            for i in range(n):
                ms = update(n * jj + i, ss[i], ms)
            return ms
        return body

    jd = (qi * tq) // tk
    ms = tuple(jnp.full((1, 2 * tq), NEG_INF, F32) for _ in range(HB))
    ms = lax.fori_loop(0, jd // 4, blocks(4), ms)
    ms = lax.cond(jd % 4 >= 2, lambda c: blocks(2)(2 * (jd // 4), c), lambda c: c, ms)
    ms = lax.cond(jd % 2 == 1, lambda c: blocks(1)(jd - 1, c), lambda c: c, ms)

    kpos = jd * tk + lax.broadcasted_iota(jnp.int32, (tk, 2 * tq), 0)
    qpos = qi * tq + (lax.broadcasted_iota(jnp.int32, (tk, 2 * tq), 1) & (tq - 1))
    ahead = jnp.maximum(kpos - qpos, 0).astype(F32)
    visible = _chunk_of(kpos) <= _chunk_of(qpos)
    ss = [jnp.where(visible, s - (2.0 * ALIBI_SLOPES[h] * LOG2E) * ahead, NEG_INF)
          for h, s in enumerate(scores(jd))]
    update(jd, ss, ms)
    for h in range(HB):
        acc = acc_ref[h, 0:LANES, :] / acc_ref[h, LANES:LANES + 1, :]
        o = (acc[:, :tq] - lam * acc[:, tq:]).T
        o = _rms(o, sg_ref[...]) * (1.0 - lam_init)
        o_ref[0, :, h * LANES:(h + 1) * LANES] = o.astype(o_ref.dtype)


def _diff_prompt(layer, q, kt_all, v_all, consts, diff_lambda, subln_g):
    b, t, w = q.shape
    tq, tk = DIFF_TQ, DIFF_TK
    assert t % tk == 0 and tk == tq and tq % CHUNK == 0 and t < 2 ** 14
    assert tq & (tq - 1) == 0
    smem = pl.BlockSpec(memory_space=pltpu.SMEM)
    qspec = pl.BlockSpec((1, tq, w), lambda i, j: (i, j, 0))
    kspec = pl.BlockSpec((1, 1, w, t), lambda i, j: (layer, i, 0, 0), pipeline_mode=pl.Buffered(1))
    vspec = pl.BlockSpec((1, t * HB, LANES), lambda i, j: (layer, i, 0), pipeline_mode=pl.Buffered(1))
    return pl.pallas_call(
        _diff_prompt_kernel,
        grid=(b, t // tq),
        in_specs=[smem, _resident(diff_lambda.shape), _resident(subln_g.shape), qspec, kspec, vspec],
        out_specs=qspec,
        out_shape=jax.ShapeDtypeStruct((b, t, w), BF16),
        scratch_shapes=[pltpu.VMEM((HB, t, 2 * LANES), BF16),
                        pltpu.VMEM((HB, t // tk, LANES + ONES_ROWS, tk), BF16),
                        pltpu.VMEM((HB, LANES + ONES_ROWS, 2 * tq), F32)],
        compiler_params=_params(2),
        name="diff_prompt",
    )(consts, diff_lambda, subln_g, q, kt_all, v_all)


def _diff_sample_kernel(consts_ref, dl_ref, sg_ref, q_ref, kt_ref, kn_ref, v_ref, vn_ref, o_ref):
    past = kt_ref.shape[4]
    s_len = q_ref.shape[1]
    lam_init = consts_ref[0]
    lam = _diff_lambda(dl_ref, lam_init)
    kpos = lax.broadcasted_iota(jnp.int32, (1, past), 1).astype(F32)
    row = lax.broadcasted_iota(jnp.int32, (2 * s_len, 1), 0)
    qloc = jnp.where(row >= s_len, row - s_len, row)
    kcol = lax.broadcasted_iota(jnp.int32, (1, s_len), 1)
    near = (past + qloc - jnp.abs(qloc - kcol)).astype(F32)
    for h in range(HB):
        lanes = slice(h * LANES, (h + 1) * LANES)
        c = ALIBI_SLOPES[h] * LOG2E
        q_lo, q_hi = _split_heads(q_ref[0, :, lanes])
        q2 = jnp.concatenate([q_lo, q_hi], axis=0)
        s_old = _dot(q2, kt_ref[0, 0, h].astype(BF16)) + c * kpos
        s_new = _dot_nt(q2, kn_ref[0, :, lanes].astype(BF16)) + c * near
        m = jnp.maximum(jnp.max(s_old, axis=-1, keepdims=True),
                        jnp.max(s_new, axis=-1, keepdims=True))
        p_old = jnp.exp2(s_old - m)
        p_new = jnp.exp2(s_new - m)
        l = jnp.sum(p_old, axis=-1, keepdims=True) + jnp.sum(p_new, axis=-1, keepdims=True)
        v_old = v_ref[0, 0, pl.ds(h, past, stride=HB), :].astype(BF16)
        acc = _dot(p_old.astype(BF16), v_old) + _dot(p_new.astype(BF16), vn_ref[0, :, lanes].astype(BF16))
        o = acc / l
        o = o[:s_len] - lam * o[s_len:]
        o = _rms(o, sg_ref[...]) * (1.0 - lam_init)
        o_ref[0, :, lanes] = o.astype(o_ref.dtype)


def _diff_sample(layer, q, k, v, cache_kt, cache_v, consts, diff_lambda, subln_g):
    b, s_len, w = q.shape
    past = cache_kt.shape[4]
    assert past % CHUNK + s_len <= CHUNK
    smem = pl.BlockSpec(memory_space=pltpu.SMEM)
    new = pl.BlockSpec((1, s_len, w), lambda i: (i, 0, 0))
    old_k = pl.BlockSpec((1, 1, HB, LANES, past), lambda i: (layer, i, 0, 0, 0))
    old_v = pl.BlockSpec((1, 1, past * HB, LANES), lambda i: (layer, i, 0, 0))
    return pl.pallas_call(
        _diff_sample_kernel,
        grid=(b,),
        in_specs=[smem, _resident(diff_lambda.shape), _resident(subln_g.shape),
                  new, old_k, new, old_v, new],
        out_specs=new,
        out_shape=jax.ShapeDtypeStruct((b, s_len, w), BF16),
        compiler_params=_params(1),
        name="diff_sample",
    )(consts, diff_lambda, subln_g, q, cache_kt, k, cache_v, v)


def _softmax_pv(s, v):
    m = jnp.max(s, axis=-1, keepdims=True)
    p = jnp.exp(s - m)
    l = jnp.sum(p, axis=-1, keepdims=True)
    return _dot(p.astype(BF16), v) / l


def _mix_kernel(x_ref, oa_ref, ob_ref, mk_ref, mv_ref, g_ref, wout_ref, wcq_ref, wco_ref, o_ref,
                *, n_batch):
    x = x_ref[...]
    wa = oa_ref.shape[1]
    rows_b = x.shape[0] // n_batch
    rows_mem = mk_ref.shape[1] // n_batch
    n_mem = rows_mem // HC
    mix = _dot(oa_ref[...], wout_ref[:wa, :]) + _dot(ob_ref[...], wout_ref[wa:, :])
    x = x + _rms(mix, g_ref[0:1])
    qc = _dot(_rms(x, g_ref[1:2]).astype(BF16), wcq_ref[...]).astype(BF16)
    outs = []
    for bi in range(n_batch):
        heads = []
        for h in range(HC):
            q = qc[bi * rows_b:(bi + 1) * rows_b, h * DHC:(h + 1) * DHC]
            mk = mk_ref[0, pl.ds(bi * rows_mem + h, n_mem, stride=HC), :].astype(BF16)
            mv = mv_ref[0, pl.ds(bi * rows_mem + h, n_mem, stride=HC), :].astype(BF16)
            heads.append(_softmax_pv(_dot_nt(q, mk) * DHC ** -0.5, mv).astype(BF16))
        outs.append(jnp.concatenate(heads, axis=-1))
    oc = _dot(outs[0] if n_batch == 1 else jnp.concatenate(outs, axis=0), wco_ref[...])
    o_ref[...] = x + _rms(oc, g_ref[2:3])


def _mix_cross(layer, x, oa, ob, mem_k, mem_v, g345, w_out, w_cq, w_co, batch):
    m, d = x.shape
    t = m // batch
    n_batch = 1 if t >= ROW_TILE else MIX_BATCHES
    tm = ROW_TILE if n_batch == 1 else n_batch * t
    assert m % tm == 0 and (t % tm == 0 or n_batch > 1)
    per = max(1, t // tm)
    rows_mem = mem_k.shape[1] // batch

    def rows(width):
        return pl.BlockSpec((tm, width), lambda i: (i, 0))

    mem = pl.BlockSpec((1, n_batch * rows_mem, LANES), lambda i: (layer, i // per, 0))
    return pl.pallas_call(
        functools.partial(_mix_kernel, n_batch=n_batch),
        grid=(m // tm,),
        in_specs=[rows(d), rows(oa.shape[1]), rows(ob.shape[1]), mem, mem, _resident(g345.shape),
                  _weight(w_out, (layer,)), _weight(w_cq, (layer,)), _weight(w_co, (layer,))],
        out_specs=rows(d),
        out_shape=jax.ShapeDtypeStruct((m, d), F32),
        compiler_params=_params(1),
        name="mix_cross",
    )(x, oa, ob, mem_k, mem_v, g345, w_out, w_cq, w_co)


def _trunk_layer(layer, depth, x, mem_k, mem_v, cache, lw, carried=None):
    b, t, d = x.shape
    g = lw["norm_g"]
    x2 = _ffn_half(x.reshape(b * t, d), g[0:1], g[1:2], lw["w_gu"], lw["w_dn"], (layer, 0))
    if cache is None:
        kt_prev, v_prev = carried if carried is not None else (None, None)
        qa, ka, va, qb, kt_all, v_all = _in_proj(layer, x2, g[2:3], lw["w_in"], lw["w_kbt"],
                                                 (depth, b, kt_prev, v_prev))
        qa, ka, va, qb = (y.reshape(b, t, -1) for y in (qa, ka, va, qb))
        oa = _band_prompt(qa, ka, va, lw["bias"])
        ob = _diff_prompt(layer, qb, kt_all, v_all, lw["consts"], lw["diff_lambda"], lw["subln_g"])
        keep = min(WINDOW_A, t)
        state = (ka[:, t - keep:], va[:, t - keep:], kt_all, v_all)
    else:
        qa, ka, va, qb, kb, vb = (y.reshape(b, t, -1)
                                  for y in _in_proj(layer, x2, g[2:3], lw["w_in"], lw["w_kbt"]))
        ca_kt, ca_vt, cb_kt, cb_v = cache
        oa = _band_sample(layer, qa, ka, va, ca_kt, ca_vt, lw["bias"])
        ob = _diff_sample(layer, qb, kb, vb, cb_kt, cb_v, lw["consts"], lw["diff_lambda"],
                          lw["subln_g"])
        state = (ka, va, kb, vb)
    x3 = _mix_cross(layer, x2, oa.reshape(b * t, -1), ob.reshape(b * t, -1), mem_k, mem_v, g[3:6],
                    lw["w_out"], lw["w_cq"], lw["w_co"], b)
    x4 = _ffn_half(x3, g[6:7], g[7:8], lw["w_gu"], lw["w_dn"], (layer, 1))
    return x4.reshape(b, t, d), state


def kernel(x_prompt, x_sample, mem_prompt, cache_a_k, cache_a_v, cache_b_k, cache_b_v, cache_mem_k,
           cache_mem_v, norm_g, w_ffn_gu, w_ffn_dn, w_in, rel_bias, diff_lambda, subln_g, w_out,
           mem_norm_g, w_cq, w_ckv, w_co):
    depth = norm_g.shape[0]
    b, t, d = x_prompt.shape
    bd, sd, _ = x_sample.shape
    n_mem = mem_prompt.shape[1]
    past_a, past_b = cache_a_k.shape[2], cache_b_k.shape[2]
    width = w_in.shape[2] // 6

    cache = (jnp.transpose(cache_a_k, (0, 1, 3, 4, 2)).reshape(depth, bd, HA * DHA, past_a),
             jnp.transpose(cache_a_v, (0, 1, 3, 4, 2)).reshape(depth, bd, HA * DHA, past_a),
             jnp.transpose(cache_b_k, (0, 1, 3, 4, 5, 2)).reshape(depth, bd, HB, 2 * DHB, past_b),
             cache_b_v.reshape(depth, bd, past_b * HB, 2 * DHB))
    smem_k = cache_mem_k.reshape(depth, bd * n_mem * HC, DHC)
    smem_v = cache_mem_v.reshape(depth, bd * n_mem * HC, DHC)

    weights = dict(
        w_gu=w_ffn_gu.astype(BF16), w_dn=w_ffn_dn.astype(BF16), w_in=w_in.astype(BF16),
        w_kbt=jnp.transpose(w_in[:, :, 4 * width:5 * width], (0, 2, 1)).astype(BF16),
        w_out=w_out.astype(BF16), w_cq=w_cq.astype(BF16), w_co=w_co.astype(BF16))
    w_ckv = w_ckv.astype(BF16)

    xp, xs = x_prompt, x_sample
    p_a, s_state = [], []
    pmem = carried = None
    for l in range(depth):
        lam_init = 0.8 - 0.6 * math.exp(-0.3 * l)
        lw = dict(weights, norm_g=norm_g[l], bias=_band_bias(rel_bias[l]),
                  consts=jnp.asarray([lam_init], F32), diff_lambda=diff_lambda[l],
                  subln_g=subln_g[l].reshape(1, -1))
        pmem = _memory_kv(l, depth, mem_prompt.reshape(b * n_mem, d), mem_norm_g[l].reshape(1, d),
                          w_ckv, pmem)
        xp, st_p = _trunk_layer(l, depth, xp, pmem[0], pmem[1], None, lw, carried)
        xs, st_s = _trunk_layer(l, depth, xs, smem_k, smem_v, cache, lw)
        carried = st_p[2:]
        p_a.append(st_p[:2])
        s_state.append(st_s)

    def stack(items, idx, shape):
        return jnp.stack([it[idx] for it in items]).reshape((depth,) + shape)

    keep = min(WINDOW_A, t)
    kt_all, v_all = carried
    return (xp, xs,
            stack(p_a, 0, (b, keep, HA, DHA)), stack(p_a, 1, (b, keep, HA, DHA)),
            jnp.transpose(kt_all.reshape(depth, b, HB, 2, DHB, t), (0, 1, 5, 2, 3, 4)),
            v_all.reshape(depth, b, t, HB, 2 * DHB),
            pmem[0].reshape(depth, b, n_mem, HC, DHC), pmem[1].reshape(depth, b, n_mem, HC, DHC),
            stack(s_state, 0, (bd, sd, HA, DHA)), stack(s_state, 1, (bd, sd, HA, DHA)),
            stack(s_state, 2, (bd, sd, HB, 2, DHB)), stack(s_state, 3, (bd, sd, HB, 2 * DHB)))
```

```python
import functools
import math

import jax
import jax.numpy as jnp
import numpy as np
from jax import lax
from jax.experimental import pallas as pl
from jax.experimental.pallas import tpu as pltpu

F32 = jnp.float32
BF16 = jnp.bfloat16

CHUNK = 64
BAND_CHUNKS = 8
WINDOW_A = BAND_CHUNKS * CHUNK
BAND = WINDOW_A + CHUNK
HA, DHA = 8, 64
HB, DHB = 4, 64
HC, DHC = 4, 128
REL_CLIP = 128
EPS = 1e-6
NEG_INF = -1e30
LOG2E = math.log2(math.e)
ALIBI_SLOPES = tuple(2.0 ** (-8.0 * (i + 1) / HB) for i in range(HB))

LANES = 128
VMEM_LIMIT = 56 * 1024 * 1024
ROW_TILE = 1024
FF_CHUNK = 256
MIX_BATCHES = 8
BAND_UNROLL = 2
DIFF_TQ = 256
DIFF_TK = 256
ONES_ROWS = 16
N_SLOPE_TERMS = 4


def _params(n_axes):
    return pltpu.CompilerParams(
        dimension_semantics=("arbitrary",) * n_axes, vmem_limit_bytes=VMEM_LIMIT)


def _rms(x, g):
    ms = jnp.mean(x * x, axis=-1, keepdims=True)
    return x * lax.rsqrt(ms + EPS) * g


def _dot(a, b):
    return jnp.dot(a, b, preferred_element_type=F32)


def _dot_nt(a, b):
    return lax.dot_general(a, b, (((1,), (1,)), ((), ())), preferred_element_type=F32)


def _chunk_of(pos):
    assert CHUNK & (CHUNK - 1) == 0
    return lax.shift_right_logical(pos, jnp.int32(CHUNK.bit_length() - 1))


def _resident(shape):
    zeros = (0,) * len(shape)
    return pl.BlockSpec(shape, lambda *_: zeros, pipeline_mode=pl.Buffered(1))


def _weight(w, lead):
    index = tuple(lead) + (0,) * (w.ndim - len(lead))
    return pl.BlockSpec((None,) * len(lead) + w.shape[len(lead):], lambda *_: index,
                        pipeline_mode=pl.Buffered(1))


def _ffn_kernel(x_ref, gpre_ref, gpost_ref, wgu_ref, wdn_ref, o_ref, acc_ref, *, d_ff):
    x = x_ref[...]
    xn = _rms(x, gpre_ref[...]).astype(BF16)
    for c in range(d_ff // FF_CHUNK):
        lo = c * FF_CHUNK
        gate = _dot(xn, wgu_ref[:, lo:lo + FF_CHUNK])
        up = _dot(xn, wgu_ref[:, d_ff + lo:d_ff + lo + FF_CHUNK])
        hidden = (gate * jax.nn.sigmoid(gate) * up).astype(BF16)
        part = _dot(hidden, wdn_ref[lo:lo + FF_CHUNK, :])
        if c == 0:
            acc_ref[...] = part
        else:
            acc_ref[...] += part
    o_ref[...] = x + 0.5 * _rms(acc_ref[...], gpost_ref[...])


def _ffn_half(x, g_pre, g_post, w_gu, w_dn, lead):
    m, d = x.shape
    d_ff = w_dn.shape[-2]
    assert d_ff % FF_CHUNK == 0 and m % ROW_TILE == 0
    row = pl.BlockSpec((ROW_TILE, d), lambda i: (i, 0))
    return pl.pallas_call(
        functools.partial(_ffn_kernel, d_ff=d_ff),
        grid=(m // ROW_TILE,),
        in_specs=[row, _resident((1, d)), _resident((1, d)), _weight(w_gu, lead), _weight(w_dn, lead)],
        out_specs=row,
        out_shape=jax.ShapeDtypeStruct((m, d), F32),
        scratch_shapes=[pltpu.VMEM((ROW_TILE, d), F32)],
        compiler_params=_params(1),
        name="ffn_half",
    )(x, g_pre, g_post, w_gu, w_dn)


def _store_head_rows(o_ref, y):
    m, heads = y.shape[0], y.shape[1] // LANES
    for h in range(heads):
        o_ref[0, pl.ds(h, m, stride=heads), :] = y[:, h * LANES:(h + 1) * LANES]


def _proj_kernel(*refs, width, state_layout, n_alias, per, keep):
    n_in = 4 + (1 if state_layout else 0)
    x_ref, g_ref, w_ref, wkt_ref = refs[:4]
    outs = refs[n_in + n_alias:]
    qa_ref, ka_ref, va_ref, qb_ref, kb_ref, vb_ref = outs[:6]
    u = _rms(x_ref[...], g_ref[...]).astype(BF16)
    qa_ref[...] = _dot(u, w_ref[:, 0:width]) * (DHA ** -0.5 * LOG2E)
    ka_ref[...] = _dot(u, w_ref[:, width:2 * width])
    va_ref[...] = _dot(u, w_ref[:, 2 * width:3 * width])
    qb_ref[...] = _dot(u, w_ref[:, 3 * width:4 * width]) * (DHB ** -0.5 * LOG2E)
    vb = _dot(u, w_ref[:, 5 * width:6 * width])
    if state_layout:
        wat_ref = refs[4]
        kat_ref, vat_ref = outs[6:]
        kb_ref[0, 0] = _dot_nt(wkt_ref[...], u)
        _store_head_rows(vb_ref, vb)

        @pl.when(pl.program_id(0) % per == per - 1)
        def _():
            tail = u[u.shape[0] - keep:, :]
            kat_ref[0, 0] = _dot_nt(wat_ref[0], tail)
            vat_ref[0, 0] = _dot_nt(wat_ref[1], tail)
    else:
        kb_ref[...] = _dot(u, w_ref[:, 4 * width:5 * width])
        vb_ref[...] = vb


def _in_proj(layer, x, g, w_in, w_kbt, w_at=None, state=None):
    m, d = x.shape
    width = w_in.shape[-1] // 6
    row = pl.BlockSpec((ROW_TILE, d), lambda i: (i, 0))
    out = pl.BlockSpec((ROW_TILE, width), lambda i: (i, 0))
    f32_rows = jax.ShapeDtypeStruct((m, width), F32)
    in_specs = [row, _resident((1, d)), _weight(w_in, (layer,)), _weight(w_kbt, (layer,))]
    args = [x, g, w_in, w_kbt]
    aliases = {}
    per = keep = 0
    if state is None:
        out_specs, out_shape = [out] * 6, [f32_rows] * 6
    else:
        depth, batch, buffers = state
        t = m // batch
        per = t // ROW_TILE
        keep = min(WINDOW_A, t)
        assert t % ROW_TILE == 0 and keep <= ROW_TILE
        in_specs.append(_weight(w_at, (layer,)))
        args.append(w_at)
        kt_spec = pl.BlockSpec((1, 1, width, ROW_TILE), lambda i: (layer, i // per, 0, i % per))
        v_spec = pl.BlockSpec((1, ROW_TILE * HB, LANES), lambda i: (layer, i, 0))
        tail_spec = pl.BlockSpec((1, 1, width, keep), lambda i: (layer, i // per, 0, 0))
        tail_shape = jax.ShapeDtypeStruct((depth, batch, width, keep), F32)
        out_specs = [out] * 4 + [kt_spec, v_spec, tail_spec, tail_spec]
        out_shape = [f32_rows] * 4 + [jax.ShapeDtypeStruct((depth, batch, width, t), F32),
                                      jax.ShapeDtypeStruct((depth, m * HB, LANES), F32),
                                      tail_shape, tail_shape]
        if buffers is not None:
            in_specs += [pl.BlockSpec(memory_space=pl.ANY)] * 4
            args += list(buffers)
            aliases = {5 + n: 4 + n for n in range(4)}
    return pl.pallas_call(
        functools.partial(_proj_kernel, width=width, state_layout=state is not None,
                          n_alias=len(aliases), per=per, keep=keep),
        grid=(m // ROW_TILE,),
        in_specs=in_specs,
        out_specs=out_specs,
        out_shape=out_shape,
        input_output_aliases=aliases,
        compiler_params=_params(1),
        name="in_proj",
    )(*args)


def _memkv_kernel(*refs, width, n_alias):
    x_ref, g_ref, w_ref = refs[:3]
    k_ref, v_ref = refs[3 + n_alias:]
    u = _rms(x_ref[...], g_ref[...]).astype(BF16)
    _store_head_rows(k_ref, _dot(u, w_ref[:, :width]))
    _store_head_rows(v_ref, _dot(u, w_ref[:, width:]))


def _memory_kv(layer, depth, mem, g, w_ckv, prev=None):
    m, d = mem.shape
    width = w_ckv.shape[-1] // 2
    row = pl.BlockSpec((ROW_TILE, d), lambda i: (i, 0))
    out = pl.BlockSpec((1, ROW_TILE * HC, LANES), lambda i: (layer, i, 0))
    in_specs = [row, _resident((1, d)), _weight(w_ckv, (layer,))]
    args = [mem, g, w_ckv]
    aliases = {}
    if prev is not None:
        in_specs += [pl.BlockSpec(memory_space=pl.ANY)] * 2
        args += list(prev)
        aliases = {3: 0, 4: 1}
    return pl.pallas_call(
        functools.partial(_memkv_kernel, width=width, n_alias=len(aliases)),
        grid=(m // ROW_TILE,),
        in_specs=in_specs,
        out_specs=[out] * 2,
        out_shape=[jax.ShapeDtypeStruct((depth, m * HC, LANES), F32)] * 2,
        input_output_aliases=aliases,
        compiler_params=_params(1),
        name="memory_kv",
    )(*args)


def _bias_kernel(table_ref, o_ref):
    pair = pl.program_id(0)
    row = lax.broadcasted_iota(jnp.int32, (2 * CHUNK, BAND), 0)
    j = lax.broadcasted_iota(jnp.int32, (2 * CHUNK, BAND), 1)
    odd = row >= CHUNK
    idx = jnp.clip((row & (CHUNK - 1)) - (j - WINDOW_A), -REL_CLIP, REL_CLIP) + REL_CLIP

    def pick(r, acc):
        return jnp.where(idx == r, jnp.where(odd, table_ref[2 * pair + 1, r], table_ref[2 * pair, r]), acc)

    first = max(0, REL_CLIP - CHUNK + 1)
    o_ref[0] = lax.fori_loop(first, 2 * REL_CLIP + 1, pick, jnp.zeros((2 * CHUNK, BAND), F32)) * LOG2E


def _band_bias(table):
    return pl.pallas_call(
        _bias_kernel,
        grid=(HA // 2,),
        in_specs=[pl.BlockSpec(memory_space=pltpu.SMEM)],
        out_specs=pl.BlockSpec((1, 2 * CHUNK, BAND), lambda p: (p, 0, 0)),
        out_shape=jax.ShapeDtypeStruct((HA // 2, 2 * CHUNK, BAND), F32),
        compiler_params=_params(1),
        name="band_bias",
    )(table)


def _split_heads(x):
    lane = lax.broadcasted_iota(jnp.int32, x.shape, 1)
    lo = jnp.where(lane < LANES // 2, x, 0.0).astype(BF16)
    hi = jnp.where(lane >= LANES // 2, x, 0.0).astype(BF16)
    return lo, hi


def _stack_heads(x):
    return jnp.concatenate(_split_heads(x), axis=0)


def _unstack_heads(o):
    m = o.shape[0] // 2
    lane = lax.broadcasted_iota(jnp.int32, (m, LANES), 1)
    return jnp.where(lane < LANES // 2, o[:m], o[m:])


def _exp2_rows(s):
    p = jnp.exp2(s - jnp.max(s, axis=-1, keepdims=True))
    return p.astype(BF16), jnp.sum(p, axis=-1, keepdims=True)


def _band_prompt_kernel(q_ref, kp_ref, kc_ref, vp_ref, vc_ref, bias_ref, o_ref, kcat_ref, vcat_ref):
    t = pl.program_id(1)
    tile = kc_ref.shape[1]
    kcat_ref[0:tile, :] = kp_ref[0].astype(BF16)
    kcat_ref[tile:2 * tile, :] = kc_ref[0].astype(BF16)
    vcat_ref[0:tile, :] = vp_ref[0].astype(BF16)
    vcat_ref[tile:2 * tile, :] = vc_ref[0].astype(BF16)
    col = lax.broadcasted_iota(jnp.int32, (2 * CHUNK, BAND), 1)

    def step(i, carry):
        work = []
        for sub in range(BAND_UNROLL):
            r0 = pl.multiple_of((i * BAND_UNROLL + sub) * CHUNK, CHUNK)
            valid = (col >= WINDOW_A - r0) | (t > 0)
            for pair in range(HA // 2):
                lanes = slice(pair * LANES, (pair + 1) * LANES)
                q2 = _stack_heads(q_ref[0, pl.ds(r0, CHUNK), lanes])
                work.append((r0, pair, lanes, valid, _dot_nt(q2, kcat_ref[pl.ds(r0, BAND), lanes])))
        probs = []
        for r0, pair, lanes, valid, s in work:
            probs.append((r0, lanes) + _exp2_rows(jnp.where(valid, s + bias_ref[pair], NEG_INF)))
        for r0, lanes, p, l in probs:
            o = _unstack_heads(_dot(p, vcat_ref[pl.ds(r0, BAND), lanes]) / l)
            o_ref[0, pl.ds(r0, CHUNK), lanes] = o.astype(o_ref.dtype)
        return carry

    lax.fori_loop(0, tile // (CHUNK * BAND_UNROLL), step, 0)


def _band_prompt(q, k, v, bias):
    b, t, w = q.shape
    tile = WINDOW_A
    assert t % tile == 0
    cur = pl.BlockSpec((1, tile, w), lambda i, j: (i, j, 0))
    prev = pl.BlockSpec((1, tile, w), lambda i, j: (i, jnp.maximum(j - 1, 0), 0))
    return pl.pallas_call(
        _band_prompt_kernel,
        grid=(b, t // tile),
        in_specs=[cur, prev, cur, prev, cur, _resident(bias.shape)],
        out_specs=cur,
        out_shape=jax.ShapeDtypeStruct((b, t, w), BF16),
        scratch_shapes=[pltpu.VMEM((2 * tile, w), BF16), pltpu.VMEM((2 * tile, w), BF16)],
        compiler_params=_params(2),
        name="band_prompt",
    )(q, k, k, v, v, bias)


def _band_sample_kernel(q_ref, kt_ref, kn_ref, vt_ref, vn_ref, bias_ref, o_ref):
    past = kt_ref.shape[3]
    s_len = q_ref.shape[1]
    work = []
    for pair in range(HA // 2):
        lanes = slice(pair * LANES, (pair + 1) * LANES)
        q2 = _stack_heads(q_ref[0, :, lanes])
        s_old = _dot(q2, kt_ref[0, 0, lanes, :].astype(BF16))
        s_new = _dot_nt(q2, kn_ref[0, :, lanes].astype(BF16))
        work.append((pair, lanes, s_old, s_new))
    probs = []
    for pair, lanes, s_old, s_new in work:
        bias = jnp.concatenate([bias_ref[pair, 0:s_len, :], bias_ref[pair, CHUNK:CHUNK + s_len, :]],
                               axis=0)
        s_old = s_old + bias[:, 0:past]
        s_new = s_new + bias[:, past:past + s_len]
        m = jnp.maximum(jnp.max(s_old, axis=-1, keepdims=True), jnp.max(s_new, axis=-1, keepdims=True))
        p_old = jnp.exp2(s_old - m)
        p_new = jnp.exp2(s_new - m)
        l = jnp.sum(p_old, axis=-1, keepdims=True) + jnp.sum(p_new, axis=-1, keepdims=True)
        probs.append((lanes, p_old.astype(BF16), p_new.astype(BF16), l))
    for lanes, p_old, p_new, l in probs:
        o = _dot_nt(p_old, vt_ref[0, 0, lanes, :].astype(BF16)) + _dot(p_new, vn_ref[0, :, lanes].astype(BF16))
        o_ref[0, :, lanes] = _unstack_heads(o / l).astype(o_ref.dtype)


def _band_sample(layer, q, k, v, cache_kt, cache_vt, bias):
    b, s_len, w = q.shape
    past = cache_kt.shape[3]
    assert past == WINDOW_A and s_len <= CHUNK
    new = pl.BlockSpec((1, s_len, w), lambda i: (i, 0, 0))
    old = pl.BlockSpec((1, 1, w, past), lambda i: (layer, i, 0, 0))
    return pl.pallas_call(
        _band_sample_kernel,
        grid=(b,),
        in_specs=[new, old, new, old, new, _resident(bias.shape)],
        out_specs=new,
        out_shape=jax.ShapeDtypeStruct((b, s_len, w), BF16),
        compiler_params=_params(1),
        name="band_sample",
    )(q, cache_kt, k, cache_vt, v, bias)


def _diff_lambda(dl_ref, lam_init):
    lp = dl_ref[...]
    a = jnp.sum(lp[0:1] * lp[1:2], axis=-1, keepdims=True)
    b = jnp.sum(lp[2:3] * lp[3:4], axis=-1, keepdims=True)
    return jnp.exp(a) - jnp.exp(b) + lam_init


def _bf16_terms(x):
    terms, rest = [], float(x)
    for _ in range(N_SLOPE_TERMS):
        term = float(np.asarray(rest, np.float32).astype(BF16))
        terms.append(term)
        rest -= term
    return terms


def _key_position_features(kpos, lane):
    hi = lax.shift_left(_chunk_of(kpos), jnp.int32(CHUNK.bit_length() - 1))
    lo = kpos & (CHUNK - 1)
    feat = jnp.where((lane & 1) == 0, hi, lo)
    return jnp.where(lane < 2 * N_SLOPE_TERMS, feat, 0).astype(F32).astype(BF16)


def _slope_features(head, lane):
    feat = jnp.zeros(lane.shape, F32)
    for i, term in enumerate(_bf16_terms(ALIBI_SLOPES[head] * LOG2E)):
        feat = jnp.where(lax.shift_right_logical(lane, jnp.int32(1)) == i, term, feat)
    return feat.astype(BF16)


def _diff_prompt_kernel(consts_ref, dl_ref, sg_ref, q_ref, kt_ref, v_ref, o_ref,
                        kaug_ref, vt_ref, acc_ref):
    qi = pl.program_id(1)
    tq, tk = DIFF_TQ, DIFF_TK
    t = kt_ref.shape[3]
    lam_init = consts_ref[0]
    lam = _diff_lambda(dl_ref, lam_init)

    @pl.when(qi == 0)
    def _():
        lane = lax.broadcasted_iota(jnp.int32, (tk, LANES), 1)
        row = lax.broadcasted_iota(jnp.int32, (tk, LANES), 0)

        def build(blk, carry):
            r0 = pl.multiple_of(blk * tk, tk)
            feat = _key_position_features(r0 + row, lane)
            for h in range(HB):
                k = kt_ref[0, 0, h * LANES:(h + 1) * LANES, pl.ds(r0, tk)]
                v = v_ref[0, pl.ds(r0 * HB + h, tk, stride=HB), :]
                kaug_ref[h, pl.ds(r0, tk), 0:LANES] = k.T.astype(BF16)
                kaug_ref[h, pl.ds(r0, tk), LANES:2 * LANES] = feat
                vt_ref[h, blk, 0:LANES, :] = v.T.astype(BF16)
                vt_ref[h, blk, LANES:LANES + ONES_ROWS, :] = jnp.ones((ONES_ROWS, tk), BF16)
            return carry

        lax.fori_loop(0, t // tk, build, 0)

    q = q_ref[0]
    qlane = lax.broadcasted_iota(jnp.int32, (tq, LANES), 1)
    qd = []
    for h in range(HB):
        q_lo, q_hi = _split_heads(q[:, h * LANES:(h + 1) * LANES])
        feat = _slope_features(h, qlane)
        qd.append(jnp.concatenate([jnp.concatenate([q_lo, feat], axis=1),
                                   jnp.concatenate([q_hi, feat], axis=1)], axis=0))
        acc_ref[h] = jnp.zeros(acc_ref.shape[1:], F32)

    def scores(j):
        off = pl.multiple_of(j * tk, tk)
        return [_dot_nt(kaug_ref[h, pl.ds(off, tk), :], qd[h]) for h in range(HB)]

    def update(j, ss, ms):
        out = []
        for h in range(HB):
            m_new = jnp.maximum(ms[h], jnp.max(ss[h], axis=0, keepdims=True))
            p = jnp.exp2(ss[h] - m_new).astype(BF16)
            acc_ref[h] = jnp.exp2(ms[h] - m_new) * acc_ref[h] + _dot(vt_ref[h, j], p)
            out.append(m_new)
        return tuple(out)

    def blocks(n):
        def body(jj, ms):
            ss = [scores(n * jj + i) for i in range(n)]
            for i in range(n):
                ms = update(n * jj + i, ss[i], ms)
            return ms
        return body

    jd = (qi * tq) // tk
    ms = tuple(jnp.full((1, 2 * tq), NEG_INF, F32) for _ in range(HB))
    ms = lax.fori_loop(0, jd // 4, blocks(4), ms)
    ms = lax.cond(jd % 4 >= 2, lambda c: blocks(2)(2 * (jd // 4), c), lambda c: c, ms)
    ms = lax.cond(jd % 2 == 1, lambda c: blocks(1)(jd - 1, c), lambda c: c, ms)

    kpos = jd * tk + lax.broadcasted_iota(jnp.int32, (tk, 2 * tq), 0)
    qpos = qi * tq + (lax.broadcasted_iota(jnp.int32, (tk, 2 * tq), 1) & (tq - 1))
    ahead = jnp.maximum(kpos - qpos, 0).astype(F32)
    visible = _chunk_of(kpos) <= _chunk_of(qpos)
    ss = [jnp.where(visible, s - (2.0 * ALIBI_SLOPES[h] * LOG2E) * ahead, NEG_INF)
          for h, s in enumerate(scores(jd))]
    update(jd, ss, ms)
    for h in range(HB):
        acc = acc_ref[h, 0:LANES, :] / acc_ref[h, LANES:LANES + 1, :]
        o = (acc[:, :tq] - lam * acc[:, tq:]).T
        o = _rms(o, sg_ref[...]) * (1.0 - lam_init)
        o_ref[0, :, h * LANES:(h + 1) * LANES] = o.astype(o_ref.dtype)


def _diff_prompt(layer, q, kt_all, v_all, consts, diff_lambda, subln_g):
    b, t, w = q.shape
    tq, tk = DIFF_TQ, DIFF_TK
    assert t % tk == 0 and tk == tq and tq % CHUNK == 0 and t < 2 ** 14
    assert tq & (tq - 1) == 0
    smem = pl.BlockSpec(memory_space=pltpu.SMEM)
    qspec = pl.BlockSpec((1, tq, w), lambda i, j: (i, j, 0))
    kspec = pl.BlockSpec((1, 1, w, t), lambda i, j: (layer, i, 0, 0), pipeline_mode=pl.Buffered(1))
    vspec = pl.BlockSpec((1, t * HB, LANES), lambda i, j: (layer, i, 0), pipeline_mode=pl.Buffered(1))
    return pl.pallas_call(
        _diff_prompt_kernel,
        grid=(b, t // tq),
        in_specs=[smem, _resident(diff_lambda.shape), _resident(subln_g.shape), qspec, kspec, vspec],
        out_specs=qspec,
        out_shape=jax.ShapeDtypeStruct((b, t, w), BF16),
        scratch_shapes=[pltpu.VMEM((HB, t, 2 * LANES), BF16),
                        pltpu.VMEM((HB, t // tk, LANES + ONES_ROWS, tk), BF16),
                        pltpu.VMEM((HB, LANES + ONES_ROWS, 2 * tq), F32)],
        compiler_params=_params(2),
        name="diff_prompt",
    )(consts, diff_lambda, subln_g, q, kt_all, v_all)


def _diff_sample_kernel(consts_ref, dl_ref, sg_ref, q_ref, kt_ref, kn_ref, v_ref, vn_ref, o_ref):
    past = kt_ref.shape[4]
    s_len = q_ref.shape[1]
    lam_init = consts_ref[0]
    lam = _diff_lambda(dl_ref, lam_init)
    kpos = lax.broadcasted_iota(jnp.int32, (1, past), 1).astype(F32)
    row = lax.broadcasted_iota(jnp.int32, (2 * s_len, 1), 0)
    qloc = jnp.where(row >= s_len, row - s_len, row)
    kcol = lax.broadcasted_iota(jnp.int32, (1, s_len), 1)
    near = (past + qloc - jnp.abs(qloc - kcol)).astype(F32)
    for h in range(HB):
        lanes = slice(h * LANES, (h + 1) * LANES)
        c = ALIBI_SLOPES[h] * LOG2E
        q_lo, q_hi = _split_heads(q_ref[0, :, lanes])
        q2 = jnp.concatenate([q_lo, q_hi], axis=0)
        s_old = _dot(q2, kt_ref[0, 0, h].astype(BF16)) + c * kpos
        s_new = _dot_nt(q2, kn_ref[0, :, lanes].astype(BF16)) + c * near
        m = jnp.maximum(jnp.max(s_old, axis=-1, keepdims=True),
                        jnp.max(s_new, axis=-1, keepdims=True))
        p_old = jnp.exp2(s_old - m)
        p_new = jnp.exp2(s_new - m)
        l = jnp.sum(p_old, axis=-1, keepdims=True) + jnp.sum(p_new, axis=-1, keepdims=True)
        v_old = v_ref[0, 0, pl.ds(h, past, stride=HB), :].astype(BF16)
        acc = _dot(p_old.astype(BF16), v_old) + _dot(p_new.astype(BF16), vn_ref[0, :, lanes].astype(BF16))
        o = acc / l
        o = o[:s_len] - lam * o[s_len:]
        o = _rms(o, sg_ref[...]) * (1.0 - lam_init)
        o_ref[0, :, lanes] = o.astype(o_ref.dtype)


def _diff_sample(layer, q, k, v, cache_kt, cache_v, consts, diff_lambda, subln_g):
    b, s_len, w = q.shape
    past = cache_kt.shape[4]
    assert past % CHUNK + s_len <= CHUNK
    smem = pl.BlockSpec(memory_space=pltpu.SMEM)
    new = pl.BlockSpec((1, s_len, w), lambda i: (i, 0, 0))
    old_k = pl.BlockSpec((1, 1, HB, LANES, past), lambda i: (layer, i, 0, 0, 0))
    old_v = pl.BlockSpec((1, 1, past * HB, LANES), lambda i: (layer, i, 0, 0))
    return pl.pallas_call(
        _diff_sample_kernel,
        grid=(b,),
        in_specs=[smem, _resident(diff_lambda.shape), _resident(subln_g.shape),
                  new, old_k, new, old_v, new],
        out_specs=new,
        out_shape=jax.ShapeDtypeStruct((b, s_len, w), BF16),
        compiler_params=_params(1),
        name="diff_sample",
    )(consts, diff_lambda, subln_g, q, cache_kt, k, cache_v, v)


def _softmax_pv(s, v):
    m = jnp.max(s, axis=-1, keepdims=True)
    p = jnp.exp(s - m)
    l = jnp.sum(p, axis=-1, keepdims=True)
    return _dot(p.astype(BF16), v) / l


def _mix_kernel(x_ref, oa_ref, ob_ref, mk_ref, mv_ref, g_ref, wout_ref, wcq_ref, wco_ref, o_ref,
                *, n_batch):
    x = x_ref[...]
    wa = oa_ref.shape[1]
    rows_b = x.shape[0] // n_batch
    rows_mem = mk_ref.shape[1] // n_batch
    n_mem = rows_mem // HC
    mix = _dot(oa_ref[...], wout_ref[:wa, :]) + _dot(ob_ref[...], wout_ref[wa:, :])
    x = x + _rms(mix, g_ref[0:1])
    qc = _dot(_rms(x, g_ref[1:2]).astype(BF16), wcq_ref[...]).astype(BF16)
    outs = []
    for bi in range(n_batch):
        heads = []
        for h in range(HC):
            q = qc[bi * rows_b:(bi + 1) * rows_b, h * DHC:(h + 1) * DHC]
            mk = mk_ref[0, pl.ds(bi * rows_mem + h, n_mem, stride=HC), :].astype(BF16)
            mv = mv_ref[0, pl.ds(bi * rows_mem + h, n_mem, stride=HC), :].astype(BF16)
            heads.append(_softmax_pv(_dot_nt(q, mk) * DHC ** -0.5, mv).astype(BF16))
        outs.append(jnp.concatenate(heads, axis=-1))
    oc = _dot(outs[0] if n_batch == 1 else jnp.concatenate(outs, axis=0), wco_ref[...])
    o_ref[...] = x + _rms(oc, g_ref[2:3])


def _mix_cross(layer, x, oa, ob, mem_k, mem_v, g345, w_out, w_cq, w_co, batch):
    m, d = x.shape
    t = m // batch
    n_batch = 1 if t >= ROW_TILE else MIX_BATCHES
    tm = ROW_TILE if n_batch == 1 else n_batch * t
    assert m % tm == 0 and (t % tm == 0 or n_batch > 1)
    per = max(1, t // tm)
    rows_mem = mem_k.shape[1] // batch

    def rows(width):
        return pl.BlockSpec((tm, width), lambda i: (i, 0))

    mem = pl.BlockSpec((1, n_batch * rows_mem, LANES), lambda i: (layer, i // per, 0))
    return pl.pallas_call(
        functools.partial(_mix_kernel, n_batch=n_batch),
        grid=(m // tm,),
        in_specs=[rows(d), rows(oa.shape[1]), rows(ob.shape[1]), mem, mem, _resident(g345.shape),
                  _weight(w_out, (layer,)), _weight(w_cq, (layer,)), _weight(w_co, (layer,))],
        out_specs=rows(d),
        out_shape=jax.ShapeDtypeStruct((m, d), F32),
        compiler_params=_params(1),
        name="mix_cross",
    )(x, oa, ob, mem_k, mem_v, g345, w_out, w_cq, w_co)


def _trunk_layer(layer, depth, x, mem_k, mem_v, cache, lw, carried=None):
    b, t, d = x.shape
    g = lw["norm_g"]
    x2 = _ffn_half(x.reshape(b * t, d), g[0:1], g[1:2], lw["w_gu"], lw["w_dn"], (layer, 0))
    if cache is None:
        qa, ka, va, qb, *state = _in_proj(layer, x2, g[2:3], lw["w_in"], lw["w_kbt"], lw["w_at"],
                                          (depth, b, carried))
        qa, ka, va, qb = (y.reshape(b, t, -1) for y in (qa, ka, va, qb))
        oa = _band_prompt(qa, ka, va, lw["bias"])
        ob = _diff_prompt(layer, qb, state[0], state[1], lw["consts"], lw["diff_lambda"],
                          lw["subln_g"])
    else:
        qa, ka, va, qb, kb, vb = (y.reshape(b, t, -1)
                                  for y in _in_proj(layer, x2, g[2:3], lw["w_in"], lw["w_kbt"]))
        ca_kt, ca_vt, cb_kt, cb_v = cache
        oa = _band_sample(layer, qa, ka, va, ca_kt, ca_vt, lw["bias"])
        ob = _diff_sample(layer, qb, kb, vb, cb_kt, cb_v, lw["consts"], lw["diff_lambda"],
                          lw["subln_g"])
        state = (ka, va, kb, vb)
    x3 = _mix_cross(layer, x2, oa.reshape(b * t, -1), ob.reshape(b * t, -1), mem_k, mem_v, g[3:6],
                    lw["w_out"], lw["w_cq"], lw["w_co"], b)
    x4 = _ffn_half(x3, g[6:7], g[7:8], lw["w_gu"], lw["w_dn"], (layer, 1))
    return x4.reshape(b, t, d), state


def kernel(x_prompt, x_sample, mem_prompt, cache_a_k, cache_a_v, cache_b_k, cache_b_v, cache_mem_k,
           cache_mem_v, norm_g, w_ffn_gu, w_ffn_dn, w_in, rel_bias, diff_lambda, subln_g, w_out,
           mem_norm_g, w_cq, w_ckv, w_co):
    depth = norm_g.shape[0]
    b, t, d = x_prompt.shape
    bd, sd, _ = x_sample.shape
    n_mem = mem_prompt.shape[1]
    past_a, past_b = cache_a_k.shape[2], cache_b_k.shape[2]
    width = w_in.shape[2] // 6

    cache = (jnp.transpose(cache_a_k, (0, 1, 3, 4, 2)).reshape(depth, bd, HA * DHA, past_a),
             jnp.transpose(cache_a_v, (0, 1, 3, 4, 2)).reshape(depth, bd, HA * DHA, past_a),
             jnp.transpose(cache_b_k, (0, 1, 3, 4, 5, 2)).reshape(depth, bd, HB, 2 * DHB, past_b),
             cache_b_v.reshape(depth, bd, past_b * HB, 2 * DHB))
    smem_k = cache_mem_k.reshape(depth, bd * n_mem * HC, DHC)
    smem_v = cache_mem_v.reshape(depth, bd * n_mem * HC, DHC)

    weights = dict(
        w_gu=w_ffn_gu.astype(BF16), w_dn=w_ffn_dn.astype(BF16), w_in=w_in.astype(BF16),
        w_kbt=jnp.transpose(w_in[:, :, 4 * width:5 * width], (0, 2, 1)).astype(BF16),
        w_at=jnp.transpose(w_in[:, :, width:3 * width].reshape(depth, d, 2, width),
                           (0, 2, 3, 1)).astype(BF16),
        w_out=w_out.astype(BF16), w_cq=w_cq.astype(BF16), w_co=w_co.astype(BF16))
    w_ckv = w_ckv.astype(BF16)

    xp, xs = x_prompt, x_sample
    s_state = []
    pmem = carried = None
    for l in range(depth):
        lam_init = 0.8 - 0.6 * math.exp(-0.3 * l)
        lw = dict(weights, norm_g=norm_g[l], bias=_band_bias(rel_bias[l]),
                  consts=jnp.asarray([lam_init], F32), diff_lambda=diff_lambda[l],
                  subln_g=subln_g[l].reshape(1, -1))
        pmem = _memory_kv(l, depth, mem_prompt.reshape(b * n_mem, d), mem_norm_g[l].reshape(1, d),
                          w_ckv, pmem)
        xp, st_p = _trunk_layer(l, depth, xp, pmem[0], pmem[1], None, lw, carried)
        xs, st_s = _trunk_layer(l, depth, xs, smem_k, smem_v, cache, lw)
        carried = st_p
        s_state.append(st_s)

    def stack(items, idx, shape):
        return jnp.stack([it[idx] for it in items]).reshape((depth,) + shape)

    keep = min(WINDOW_A, t)
    kt_all, v_all, kat_all, vat_all = carried
    return (xp, xs,
            jnp.transpose(kat_all.reshape(depth, b, HA, DHA, keep), (0, 1, 4, 2, 3)),
            jnp.transpose(vat_all.reshape(depth, b, HA, DHA, keep), (0, 1, 4, 2, 3)),
            jnp.transpose(kt_all.reshape(depth, b, HB, 2, DHB, t), (0, 1, 5, 2, 3, 4)),
            v_all.reshape(depth, b, t, HB, 2 * DHB),
            pmem[0].reshape(depth, b, n_mem, HC, DHC), pmem[1].reshape(depth, b, n_mem, HC, DHC),
            stack(s_state, 0, (bd, sd, HA, DHA)), stack(s_state, 1, (bd, sd, HA, DHA)),
            stack(s_state, 2, (bd, sd, HB, 2, DHB)), stack(s_state, 3, (bd, sd, HB, 2 * DHB)))
```

```python
import functools
import math

import jax
import jax.numpy as jnp
import numpy as np
from jax import lax
from jax.experimental import pallas as pl
from jax.experimental.pallas import tpu as pltpu

F32 = jnp.float32
BF16 = jnp.bfloat16

CHUNK = 64
BAND_CHUNKS = 8
WINDOW_A = BAND_CHUNKS * CHUNK
BAND = WINDOW_A + CHUNK
HA, DHA = 8, 64
HB, DHB = 4, 64
HC, DHC = 4, 128
REL_CLIP = 128
EPS = 1e-6
NEG_INF = -1e30
LOG2E = math.log2(math.e)
ALIBI_SLOPES = tuple(2.0 ** (-8.0 * (i + 1) / HB) for i in range(HB))

LANES = 128
VMEM_LIMIT = 56 * 1024 * 1024
ROW_TILE = 1024
PROJ_TILE = 512
FF_CHUNK = 256
MIX_BATCHES = 8
BAND_UNROLL = 2
DIFF_TQ = 256
DIFF_TK = 256
ONES_ROWS = 16
N_SLOPE_TERMS = 4


def _params(n_axes):
    return pltpu.CompilerParams(
        dimension_semantics=("arbitrary",) * n_axes, vmem_limit_bytes=VMEM_LIMIT)


def _rms(x, g):
    ms = jnp.mean(x * x, axis=-1, keepdims=True)
    return x * lax.rsqrt(ms + EPS) * g


def _dot(a, b):
    return jnp.dot(a, b, preferred_element_type=F32)


def _dot_nt(a, b):
    return lax.dot_general(a, b, (((1,), (1,)), ((), ())), preferred_element_type=F32)


def _chunk_of(pos):
    assert CHUNK & (CHUNK - 1) == 0
    return lax.shift_right_logical(pos, jnp.int32(CHUNK.bit_length() - 1))


def _resident(shape):
    zeros = (0,) * len(shape)
    return pl.BlockSpec(shape, lambda *_: zeros, pipeline_mode=pl.Buffered(1))


def _weight(w, lead):
    index = tuple(lead) + (0,) * (w.ndim - len(lead))
    return pl.BlockSpec((None,) * len(lead) + w.shape[len(lead):], lambda *_: index,
                        pipeline_mode=pl.Buffered(1))


def _ffn_kernel(x_ref, gpre_ref, gpost_ref, wgu_ref, wdn_ref, o_ref, acc_ref, *, d_ff):
    x = x_ref[...]
    xn = _rms(x, gpre_ref[...]).astype(BF16)
    for c in range(d_ff // FF_CHUNK):
        lo = c * FF_CHUNK
        gate = _dot(xn, wgu_ref[:, lo:lo + FF_CHUNK])
        up = _dot(xn, wgu_ref[:, d_ff + lo:d_ff + lo + FF_CHUNK])
        hidden = (gate * jax.nn.sigmoid(gate) * up).astype(BF16)
        part = _dot(hidden, wdn_ref[lo:lo + FF_CHUNK, :])
        if c == 0:
            acc_ref[...] = part
        else:
            acc_ref[...] += part
    o_ref[...] = x + 0.5 * _rms(acc_ref[...], gpost_ref[...])


def _ffn_half(x, g_pre, g_post, w_gu, w_dn, lead):
    m, d = x.shape
    d_ff = w_dn.shape[-2]
    assert d_ff % FF_CHUNK == 0 and m % ROW_TILE == 0
    row = pl.BlockSpec((ROW_TILE, d), lambda i: (i, 0))
    return pl.pallas_call(
        functools.partial(_ffn_kernel, d_ff=d_ff),
        grid=(m // ROW_TILE,),
        in_specs=[row, _resident((1, d)), _resident((1, d)), _weight(w_gu, lead), _weight(w_dn, lead)],
        out_specs=row,
        out_shape=jax.ShapeDtypeStruct((m, d), F32),
        scratch_shapes=[pltpu.VMEM((ROW_TILE, d), F32)],
        compiler_params=_params(1),
        name="ffn_half",
    )(x, g_pre, g_post, w_gu, w_dn)


def _store_head_rows(o_ref, lead, y):
    m, heads = y.shape[0], y.shape[1] // LANES
    for h in range(heads):
        o_ref[lead, pl.ds(h, m, stride=heads), :] = y[:, h * LANES:(h + 1) * LANES]


def _zero_other_layers(o_ref, lead):
    for l in range(o_ref.shape[0]):
        if l != lead:
            o_ref[l] = jnp.zeros(o_ref.shape[1:], o_ref.dtype)


def _proj_kernel(*refs, width, state_layout, n_alias, per, keep, lead):
    n_in = 4 + (1 if state_layout else 0)
    x_ref, g_ref, w_ref, wkt_ref = refs[:4]
    outs = refs[n_in + n_alias:]
    qa_ref, ka_ref, va_ref, qb_ref, kb_ref, vb_ref = outs[:6]
    u = _rms(x_ref[...], g_ref[...]).astype(BF16)
    qa_ref[...] = _dot(u, w_ref[:, 0:width]) * (DHA ** -0.5 * LOG2E)
    ka_ref[...] = _dot(u, w_ref[:, width:2 * width])
    va_ref[...] = _dot(u, w_ref[:, 2 * width:3 * width])
    qb_ref[...] = _dot(u, w_ref[:, 3 * width:4 * width]) * (DHB ** -0.5 * LOG2E)
    vb = _dot(u, w_ref[:, 5 * width:6 * width])
    if state_layout:
        wat_ref = refs[4]
        kat_ref, vat_ref = outs[6:]
        _zero_other_layers(kb_ref, lead)
        _zero_other_layers(vb_ref, lead)
        kb_ref[lead, 0] = _dot_nt(wkt_ref[...], u)
        _store_head_rows(vb_ref, lead, vb)

        @pl.when(pl.program_id(0) % per == per - 1)
        def _():
            tail = u[u.shape[0] - keep:, :]
            _zero_other_layers(kat_ref, lead)
            _zero_other_layers(vat_ref, lead)
            kat_ref[lead, 0] = _dot_nt(wat_ref[0], tail)
            vat_ref[lead, 0] = _dot_nt(wat_ref[1], tail)
    else:
        kb_ref[...] = _dot(u, w_ref[:, 4 * width:5 * width])
        vb_ref[...] = vb


def _in_proj(layer, x, g, w_in, w_kbt, w_at=None, state=None):
    m, d = x.shape
    width = w_in.shape[-1] // 6
    row = pl.BlockSpec((PROJ_TILE, d), lambda i: (i, 0))
    out = pl.BlockSpec((PROJ_TILE, width), lambda i: (i, 0))
    f32_rows = jax.ShapeDtypeStruct((m, width), F32)
    in_specs = [row, _resident((1, d)), _weight(w_in, (layer,)), _weight(w_kbt, (layer,))]
    args = [x, g, w_in, w_kbt]
    aliases = {}
    per = keep = lead = 0
    if state is None:
        out_specs, out_shape = [out] * 6, [f32_rows] * 6
    else:
        depth, batch, buffers = state
        t = m // batch
        per = t // PROJ_TILE
        keep = min(WINDOW_A, t)
        assert t % PROJ_TILE == 0 and keep <= PROJ_TILE
        in_specs.append(_weight(w_at, (layer,)))
        args.append(w_at)
        n_lead, at = (depth, 0) if buffers is None else (1, layer)
        lead = layer if buffers is None else 0
        kt_spec = pl.BlockSpec((n_lead, 1, width, PROJ_TILE), lambda i: (at, i // per, 0, i % per))
        v_spec = pl.BlockSpec((n_lead, PROJ_TILE * HB, LANES), lambda i: (at, i, 0))
        tail_spec = pl.BlockSpec((n_lead, 1, width, keep), lambda i: (at, i // per, 0, 0))
        tail_shape = jax.ShapeDtypeStruct((depth, batch, width, keep), F32)
        out_specs = [out] * 4 + [kt_spec, v_spec, tail_spec, tail_spec]
        out_shape = [f32_rows] * 4 + [jax.ShapeDtypeStruct((depth, batch, width, t), F32),
                                      jax.ShapeDtypeStruct((depth, m * HB, LANES), F32),
                                      tail_shape, tail_shape]
        if buffers is not None:
            in_specs += [pl.BlockSpec(memory_space=pl.ANY)] * 4
            args += list(buffers)
            aliases = {5 + n: 4 + n for n in range(4)}
    return pl.pallas_call(
        functools.partial(_proj_kernel, width=width, state_layout=state is not None,
                          n_alias=len(aliases), per=per, keep=keep, lead=lead),
        grid=(m // PROJ_TILE,),
        in_specs=in_specs,
        out_specs=out_specs,
        out_shape=out_shape,
        input_output_aliases=aliases,
        compiler_params=_params(1),
        name="in_proj",
    )(*args)


def _memkv_kernel(*refs, width, n_alias, lead):
    x_ref, g_ref, w_ref = refs[:3]
    k_ref, v_ref = refs[3 + n_alias:]
    u = _rms(x_ref[...], g_ref[...]).astype(BF16)
    _zero_other_layers(k_ref, lead)
    _zero_other_layers(v_ref, lead)
    _store_head_rows(k_ref, lead, _dot(u, w_ref[:, :width]))
    _store_head_rows(v_ref, lead, _dot(u, w_ref[:, width:]))


def _memory_kv(layer, depth, mem, g, w_ckv, prev=None):
    m, d = mem.shape
    width = w_ckv.shape[-1] // 2
    row = pl.BlockSpec((ROW_TILE, d), lambda i: (i, 0))
    n_lead, at, lead = (depth, 0, layer) if prev is None else (1, layer, 0)
    out = pl.BlockSpec((n_lead, ROW_TILE * HC, LANES), lambda i: (at, i, 0))
    in_specs = [row, _resident((1, d)), _weight(w_ckv, (layer,))]
    args = [mem, g, w_ckv]
    aliases = {}
    if prev is not None:
        in_specs += [pl.BlockSpec(memory_space=pl.ANY)] * 2
        args += list(prev)
        aliases = {3: 0, 4: 1}
    return pl.pallas_call(
        functools.partial(_memkv_kernel, width=width, n_alias=len(aliases), lead=lead),
        grid=(m // ROW_TILE,),
        in_specs=in_specs,
        out_specs=[out] * 2,
        out_shape=[jax.ShapeDtypeStruct((depth, m * HC, LANES), F32)] * 2,
        input_output_aliases=aliases,
        compiler_params=_params(1),
        name="memory_kv",
    )(*args)


def _bias_kernel(table_ref, o_ref):
    pair = pl.program_id(0)
    row = lax.broadcasted_iota(jnp.int32, (2 * CHUNK, BAND), 0)
    j = lax.broadcasted_iota(jnp.int32, (2 * CHUNK, BAND), 1)
    odd = row >= CHUNK
    idx = jnp.clip((row & (CHUNK - 1)) - (j - WINDOW_A), -REL_CLIP, REL_CLIP) + REL_CLIP

    def pick(r, acc):
        return jnp.where(idx == r, jnp.where(odd, table_ref[2 * pair + 1, r], table_ref[2 * pair, r]), acc)

    first = max(0, REL_CLIP - CHUNK + 1)
    o_ref[0] = lax.fori_loop(first, 2 * REL_CLIP + 1, pick, jnp.zeros((2 * CHUNK, BAND), F32)) * LOG2E


def _band_bias(table):
    return pl.pallas_call(
        _bias_kernel,
        grid=(HA // 2,),
        in_specs=[pl.BlockSpec(memory_space=pltpu.SMEM)],
        out_specs=pl.BlockSpec((1, 2 * CHUNK, BAND), lambda p: (p, 0, 0)),
        out_shape=jax.ShapeDtypeStruct((HA // 2, 2 * CHUNK, BAND), F32),
        compiler_params=_params(1),
        name="band_bias",
    )(table)


def _split_heads(x):
    lane = lax.broadcasted_iota(jnp.int32, x.shape, 1)
    lo = jnp.where(lane < LANES // 2, x, 0.0).astype(BF16)
    hi = jnp.where(lane >= LANES // 2, x, 0.0).astype(BF16)
    return lo, hi


def _stack_heads(x):
    return jnp.concatenate(_split_heads(x), axis=0)


def _unstack_heads(o):
    m = o.shape[0] // 2
    lane = lax.broadcasted_iota(jnp.int32, (m, LANES), 1)
    return jnp.where(lane < LANES // 2, o[:m], o[m:])


def _exp2_rows(s):
    p = jnp.exp2(s - jnp.max(s, axis=-1, keepdims=True))
    return p.astype(BF16), jnp.sum(p, axis=-1, keepdims=True)


def _band_prompt_kernel(q_ref, kp_ref, kc_ref, vp_ref, vc_ref, bias_ref, o_ref, kcat_ref, vcat_ref):
    t = pl.program_id(1)
    tile = kc_ref.shape[1]
    kcat_ref[0:tile, :] = kp_ref[0].astype(BF16)
    kcat_ref[tile:2 * tile, :] = kc_ref[0].astype(BF16)
    vcat_ref[0:tile, :] = vp_ref[0].astype(BF16)
    vcat_ref[tile:2 * tile, :] = vc_ref[0].astype(BF16)
    col = lax.broadcasted_iota(jnp.int32, (2 * CHUNK, BAND), 1)

    def step(i, carry):
        work = []
        for sub in range(BAND_UNROLL):
            r0 = pl.multiple_of((i * BAND_UNROLL + sub) * CHUNK, CHUNK)
            valid = (col >= WINDOW_A - r0) | (t > 0)
            for pair in range(HA // 2):
                lanes = slice(pair * LANES, (pair + 1) * LANES)
                q2 = _stack_heads(q_ref[0, pl.ds(r0, CHUNK), lanes])
                work.append((r0, pair, lanes, valid, _dot_nt(q2, kcat_ref[pl.ds(r0, BAND), lanes])))
        probs = []
        for r0, pair, lanes, valid, s in work:
            probs.append((r0, lanes) + _exp2_rows(jnp.where(valid, s + bias_ref[pair], NEG_INF)))
        for r0, lanes, p, l in probs:
            o = _unstack_heads(_dot(p, vcat_ref[pl.ds(r0, BAND), lanes]) / l)
            o_ref[0, pl.ds(r0, CHUNK), lanes] = o.astype(o_ref.dtype)
        return carry

    lax.fori_loop(0, tile // (CHUNK * BAND_UNROLL), step, 0)


def _band_prompt(q, k, v, bias):
    b, t, w = q.shape
    tile = WINDOW_A
    assert t % tile == 0
    cur = pl.BlockSpec((1, tile, w), lambda i, j: (i, j, 0))
    prev = pl.BlockSpec((1, tile, w), lambda i, j: (i, jnp.maximum(j - 1, 0), 0))
    return pl.pallas_call(
        _band_prompt_kernel,
        grid=(b, t // tile),
        in_specs=[cur, prev, cur, prev, cur, _resident(bias.shape)],
        out_specs=cur,
        out_shape=jax.ShapeDtypeStruct((b, t, w), BF16),
        scratch_shapes=[pltpu.VMEM((2 * tile, w), BF16), pltpu.VMEM((2 * tile, w), BF16)],
        compiler_params=_params(2),
        name="band_prompt",
    )(q, k, k, v, v, bias)


def _band_sample_kernel(q_ref, kt_ref, kn_ref, vt_ref, vn_ref, bias_ref, o_ref):
    past = kt_ref.shape[3]
    s_len = q_ref.shape[1]
    work = []
    for pair in range(HA // 2):
        lanes = slice(pair * LANES, (pair + 1) * LANES)
        q2 = _stack_heads(q_ref[0, :, lanes])
        s_old = _dot(q2, kt_ref[0, 0, lanes, :].astype(BF16))
        s_new = _dot_nt(q2, kn_ref[0, :, lanes].astype(BF16))
        work.append((pair, lanes, s_old, s_new))
    probs = []
    for pair, lanes, s_old, s_new in work:
        bias = jnp.concatenate([bias_ref[pair, 0:s_len, :], bias_ref[pair, CHUNK:CHUNK + s_len, :]],
                               axis=0)
        s_old = s_old + bias[:, 0:past]
        s_new = s_new + bias[:, past:past + s_len]
        m = jnp.maximum(jnp.max(s_old, axis=-1, keepdims=True), jnp.max(s_new, axis=-1, keepdims=True))
        p_old = jnp.exp2(s_old - m)
        p_new = jnp.exp2(s_new - m)
        l = jnp.sum(p_old, axis=-1, keepdims=True) + jnp.sum(p_new, axis=-1, keepdims=True)
        probs.append((lanes, p_old.astype(BF16), p_new.astype(BF16), l))
    for lanes, p_old, p_new, l in probs:
        o = _dot_nt(p_old, vt_ref[0, 0, lanes, :].astype(BF16)) + _dot(p_new, vn_ref[0, :, lanes].astype(BF16))
        o_ref[0, :, lanes] = _unstack_heads(o / l).astype(o_ref.dtype)


def _band_sample(layer, q, k, v, cache_kt, cache_vt, bias):
    b, s_len, w = q.shape
    past = cache_kt.shape[3]
    assert past == WINDOW_A and s_len <= CHUNK
    new = pl.BlockSpec((1, s_len, w), lambda i: (i, 0, 0))
    old = pl.BlockSpec((1, 1, w, past), lambda i: (layer, i, 0, 0))
    return pl.pallas_call(
        _band_sample_kernel,
        grid=(b,),
        in_specs=[new, old, new, old, new, _resident(bias.shape)],
        out_specs=new,
        out_shape=jax.ShapeDtypeStruct((b, s_len, w), BF16),
        compiler_params=_params(1),
        name="band_sample",
    )(q, cache_kt, k, cache_vt, v, bias)


def _diff_lambda(dl_ref, lam_init):
    lp = dl_ref[...]
    a = jnp.sum(lp[0:1] * lp[1:2], axis=-1, keepdims=True)
    b = jnp.sum(lp[2:3] * lp[3:4], axis=-1, keepdims=True)
    return jnp.exp(a) - jnp.exp(b) + lam_init


def _bf16_terms(x):
    terms, rest = [], float(x)
    for _ in range(N_SLOPE_TERMS):
        term = float(np.asarray(rest, np.float32).astype(BF16))
        terms.append(term)
        rest -= term
    return terms


def _key_position_features(kpos, lane):
    hi = lax.shift_left(_chunk_of(kpos), jnp.int32(CHUNK.bit_length() - 1))
    lo = kpos & (CHUNK - 1)
    feat = jnp.where((lane & 1) == 0, hi, lo)
    return jnp.where(lane < 2 * N_SLOPE_TERMS, feat, 0).astype(F32).astype(BF16)


def _slope_features(head, lane):
    feat = jnp.zeros(lane.shape, F32)
    for i, term in enumerate(_bf16_terms(ALIBI_SLOPES[head] * LOG2E)):
        feat = jnp.where(lax.shift_right_logical(lane, jnp.int32(1)) == i, term, feat)
    return feat.astype(BF16)


def _diff_prompt_kernel(consts_ref, dl_ref, sg_ref, q_ref, kt_ref, v_ref, o_ref,
                        kaug_ref, vt_ref, acc_ref):
    qi = pl.program_id(1)
    tq, tk = DIFF_TQ, DIFF_TK
    t = kt_ref.shape[3]
    lam_init = consts_ref[0]
    lam = _diff_lambda(dl_ref, lam_init)

    @pl.when(qi == 0)
    def _():
        lane = lax.broadcasted_iota(jnp.int32, (tk, LANES), 1)
        row = lax.broadcasted_iota(jnp.int32, (tk, LANES), 0)

        def build(blk, carry):
            r0 = pl.multiple_of(blk * tk, tk)
            feat = _key_position_features(r0 + row, lane)
            for h in range(HB):
                k = kt_ref[0, 0, h * LANES:(h + 1) * LANES, pl.ds(r0, tk)]
                v = v_ref[0, pl.ds(r0 * HB + h, tk, stride=HB), :]
                kaug_ref[h, pl.ds(r0, tk), 0:LANES] = k.T.astype(BF16)
                kaug_ref[h, pl.ds(r0, tk), LANES:2 * LANES] = feat
                vt_ref[h, blk, 0:LANES, :] = v.T.astype(BF16)
                vt_ref[h, blk, LANES:LANES + ONES_ROWS, :] = jnp.ones((ONES_ROWS, tk), BF16)
            return carry

        lax.fori_loop(0, t // tk, build, 0)

    q = q_ref[0]
    qlane = lax.broadcasted_iota(jnp.int32, (tq, LANES), 1)
    qd = []
    for h in range(HB):
        q_lo, q_hi = _split_heads(q[:, h * LANES:(h + 1) * LANES])
        feat = _slope_features(h, qlane)
        qd.append(jnp.concatenate([jnp.concatenate([q_lo, feat], axis=1),
                                   jnp.concatenate([q_hi, feat], axis=1)], axis=0))
        acc_ref[h] = jnp.zeros(acc_ref.shape[1:], F32)

    def scores(j):
        off = pl.multiple_of(j * tk, tk)
        return [_dot_nt(kaug_ref[h, pl.ds(off, tk), :], qd[h]) for h in range(HB)]

    def update(j, ss, ms):
        out = []
        for h in range(HB):
            m_new = jnp.maximum(ms[h], jnp.max(ss[h], axis=0, keepdims=True))
            p = jnp.exp2(ss[h] - m_new).astype(BF16)
            acc_ref[h] = jnp.exp2(ms[h] - m_new) * acc_ref[h] + _dot(vt_ref[h, j], p)
            out.append(m_new)
        return tuple(out)

    def blocks(n):
        def body(jj, ms):
            ss = [scores(n * jj + i) for i in range(n)]
            for i in range(n):
                ms = update(n * jj + i, ss[i], ms)
            return ms
        return body

    jd = (qi * tq) // tk
    ms = tuple(jnp.full((1, 2 * tq), NEG_INF, F32) for _ in range(HB))
    ms = lax.fori_loop(0, jd // 4, blocks(4), ms)
    ms = lax.cond(jd % 4 >= 2, lambda c: blocks(2)(2 * (jd // 4), c), lambda c: c, ms)
    ms = lax.cond(jd % 2 == 1, lambda c: blocks(1)(jd - 1, c), lambda c: c, ms)

    kpos = jd * tk + lax.broadcasted_iota(jnp.int32, (tk, 2 * tq), 0)
    qpos = qi * tq + (lax.broadcasted_iota(jnp.int32, (tk, 2 * tq), 1) & (tq - 1))
    ahead = jnp.maximum(kpos - qpos, 0).astype(F32)
    visible = _chunk_of(kpos) <= _chunk_of(qpos)
    ss = [jnp.where(visible, s - (2.0 * ALIBI_SLOPES[h] * LOG2E) * ahead, NEG_INF)
          for h, s in enumerate(scores(jd))]
    update(jd, ss, ms)
    for h in range(HB):
        acc = acc_ref[h, 0:LANES, :] / acc_ref[h, LANES:LANES + 1, :]
        o = (acc[:, :tq] - lam * acc[:, tq:]).T
        o = _rms(o, sg_ref[...]) * (1.0 - lam_init)
        o_ref[0, :, h * LANES:(h + 1) * LANES] = o.astype(o_ref.dtype)


def _diff_prompt(layer, q, kt_all, v_all, consts, diff_lambda, subln_g):
    b, t, w = q.shape
    tq, tk = DIFF_TQ, DIFF_TK
    assert t % tk == 0 and tk == tq and tq % CHUNK == 0 and t < 2 ** 14
    assert tq & (tq - 1) == 0
    smem = pl.BlockSpec(memory_space=pltpu.SMEM)
    qspec = pl.BlockSpec((1, tq, w), lambda i, j: (i, j, 0))
    kspec = pl.BlockSpec((1, 1, w, t), lambda i, j: (layer, i, 0, 0), pipeline_mode=pl.Buffered(1))
    vspec = pl.BlockSpec((1, t * HB, LANES), lambda i, j: (layer, i, 0), pipeline_mode=pl.Buffered(1))
    return pl.pallas_call(
        _diff_prompt_kernel,
        grid=(b, t // tq),
        in_specs=[smem, _resident(diff_lambda.shape), _resident(subln_g.shape), qspec, kspec, vspec],
        out_specs=qspec,
        out_shape=jax.ShapeDtypeStruct((b, t, w), BF16),
        scratch_shapes=[pltpu.VMEM((HB, t, 2 * LANES), BF16),
                        pltpu.VMEM((HB, t // tk, LANES + ONES_ROWS, tk), BF16),
                        pltpu.VMEM((HB, LANES + ONES_ROWS, 2 * tq), F32)],
        compiler_params=_params(2),
        name="diff_prompt",
    )(consts, diff_lambda, subln_g, q, kt_all, v_all)


def _diff_sample_kernel(consts_ref, dl_ref, sg_ref, q_ref, kt_ref, kn_ref, v_ref, vn_ref, o_ref):
    past = kt_ref.shape[4]
    s_len = q_ref.shape[1]
    lam_init = consts_ref[0]
    lam = _diff_lambda(dl_ref, lam_init)
    kpos = lax.broadcasted_iota(jnp.int32, (1, past), 1).astype(F32)
    row = lax.broadcasted_iota(jnp.int32, (2 * s_len, 1), 0)
    qloc = jnp.where(row >= s_len, row - s_len, row)
    kcol = lax.broadcasted_iota(jnp.int32, (1, s_len), 1)
    near = (past + qloc - jnp.abs(qloc - kcol)).astype(F32)
    for h in range(HB):
        lanes = slice(h * LANES, (h + 1) * LANES)
        c = ALIBI_SLOPES[h] * LOG2E
        q_lo, q_hi = _split_heads(q_ref[0, :, lanes])
        q2 = jnp.concatenate([q_lo, q_hi], axis=0)
        s_old = _dot(q2, kt_ref[0, 0, h].astype(BF16)) + c * kpos
        s_new = _dot_nt(q2, kn_ref[0, :, lanes].astype(BF16)) + c * near
        m = jnp.maximum(jnp.max(s_old, axis=-1, keepdims=True),
                        jnp.max(s_new, axis=-1, keepdims=True))
        p_old = jnp.exp2(s_old - m)
        p_new = jnp.exp2(s_new - m)
        l = jnp.sum(p_old, axis=-1, keepdims=True) + jnp.sum(p_new, axis=-1, keepdims=True)
        v_old = v_ref[0, 0, pl.ds(h, past, stride=HB), :].astype(BF16)
        acc = _dot(p_old.astype(BF16), v_old) + _dot(p_new.astype(BF16), vn_ref[0, :, lanes].astype(BF16))
        o = acc / l
        o = o[:s_len] - lam * o[s_len:]
        o = _rms(o, sg_ref[...]) * (1.0 - lam_init)
        o_ref[0, :, lanes] = o.astype(o_ref.dtype)


def _diff_sample(layer, q, k, v, cache_kt, cache_v, consts, diff_lambda, subln_g):
    b, s_len, w = q.shape
    past = cache_kt.shape[4]
    assert past % CHUNK + s_len <= CHUNK
    smem = pl.BlockSpec(memory_space=pltpu.SMEM)
    new = pl.BlockSpec((1, s_len, w), lambda i: (i, 0, 0))
    old_k = pl.BlockSpec((1, 1, HB, LANES, past), lambda i: (layer, i, 0, 0, 0))
    old_v = pl.BlockSpec((1, 1, past * HB, LANES), lambda i: (layer, i, 0, 0))
    return pl.pallas_call(
        _diff_sample_kernel,
        grid=(b,),
        in_specs=[smem, _resident(diff_lambda.shape), _resident(subln_g.shape),
                  new, old_k, new, old_v, new],
        out_specs=new,
        out_shape=jax.ShapeDtypeStruct((b, s_len, w), BF16),
        compiler_params=_params(1),
        name="diff_sample",
    )(consts, diff_lambda, subln_g, q, cache_kt, k, cache_v, v)


def _softmax_pv(s, v):
    m = jnp.max(s, axis=-1, keepdims=True)
    p = jnp.exp(s - m)
    l = jnp.sum(p, axis=-1, keepdims=True)
    return _dot(p.astype(BF16), v) / l


def _mix_kernel(x_ref, oa_ref, ob_ref, mk_ref, mv_ref, g_ref, wout_ref, wcq_ref, wco_ref, o_ref,
                *, n_batch):
    x = x_ref[...]
    wa = oa_ref.shape[1]
    rows_b = x.shape[0] // n_batch
    rows_mem = mk_ref.shape[1] // n_batch
    n_mem = rows_mem // HC
    mix = _dot(oa_ref[...], wout_ref[:wa, :]) + _dot(ob_ref[...], wout_ref[wa:, :])
    x = x + _rms(mix, g_ref[0:1])
    qc = _dot(_rms(x, g_ref[1:2]).astype(BF16), wcq_ref[...]).astype(BF16)
    outs = []
    for bi in range(n_batch):
        heads = []
        for h in range(HC):
            q = qc[bi * rows_b:(bi + 1) * rows_b, h * DHC:(h + 1) * DHC]
            mk = mk_ref[0, pl.ds(bi * rows_mem + h, n_mem, stride=HC), :].astype(BF16)
            mv = mv_ref[0, pl.ds(bi * rows_mem + h, n_mem, stride=HC), :].astype(BF16)
            heads.append(_softmax_pv(_dot_nt(q, mk) * DHC ** -0.5, mv).astype(BF16))
        outs.append(jnp.concatenate(heads, axis=-1))
    oc = _dot(outs[0] if n_batch == 1 else jnp.concatenate(outs, axis=0), wco_ref[...])
    o_ref[...] = x + _rms(oc, g_ref[2:3])


def _mix_cross(layer, x, oa, ob, mem_k, mem_v, g345, w_out, w_cq, w_co, batch):
    m, d = x.shape
    t = m // batch
    n_batch = 1 if t >= ROW_TILE else MIX_BATCHES
    tm = ROW_TILE if n_batch == 1 else n_batch * t
    assert m % tm == 0 and (t % tm == 0 or n_batch > 1)
    per = max(1, t // tm)
    rows_mem = mem_k.shape[1] // batch

    def rows(width):
        return pl.BlockSpec((tm, width), lambda i: (i, 0))

    mem = pl.BlockSpec((1, n_batch * rows_mem, LANES), lambda i: (layer, i // per, 0))
    return pl.pallas_call(
        functools.partial(_mix_kernel, n_batch=n_batch),
        grid=(m // tm,),
        in_specs=[rows(d), rows(oa.shape[1]), rows(ob.shape[1]), mem, mem, _resident(g345.shape),
                  _weight(w_out, (layer,)), _weight(w_cq, (layer,)), _weight(w_co, (layer,))],
        out_specs=rows(d),
        out_shape=jax.ShapeDtypeStruct((m, d), F32),
        compiler_params=_params(1),
        name="mix_cross",
    )(x, oa, ob, mem_k, mem_v, g345, w_out, w_cq, w_co)


def _trunk_layer(layer, depth, x, mem_k, mem_v, cache, lw, carried=None):
    b, t, d = x.shape
    g = lw["norm_g"]
    x2 = _ffn_half(x.reshape(b * t, d), g[0:1], g[1:2], lw["w_gu"], lw["w_dn"], (layer, 0))
    if cache is None:
        qa, ka, va, qb, *state = _in_proj(layer, x2, g[2:3], lw["w_in"], lw["w_kbt"], lw["w_at"],
                                          (depth, b, carried))
        qa, ka, va, qb = (y.reshape(b, t, -1) for y in (qa, ka, va, qb))
        oa = _band_prompt(qa, ka, va, lw["bias"])
        ob = _diff_prompt(layer, qb, state[0], state[1], lw["consts"], lw["diff_lambda"],
                          lw["subln_g"])
    else:
        qa, ka, va, qb, kb, vb = (y.reshape(b, t, -1)
                                  for y in _in_proj(layer, x2, g[2:3], lw["w_in"], lw["w_kbt"]))
        ca_kt, ca_vt, cb_kt, cb_v = cache
        oa = _band_sample(layer, qa, ka, va, ca_kt, ca_vt, lw["bias"])
        ob = _diff_sample(layer, qb, kb, vb, cb_kt, cb_v, lw["consts"], lw["diff_lambda"],
                          lw["subln_g"])
        state = (ka, va, kb, vb)
    x3 = _mix_cross(layer, x2, oa.reshape(b * t, -1), ob.reshape(b * t, -1), mem_k, mem_v, g[3:6],
                    lw["w_out"], lw["w_cq"], lw["w_co"], b)
    x4 = _ffn_half(x3, g[6:7], g[7:8], lw["w_gu"], lw["w_dn"], (layer, 1))
    return x4.reshape(b, t, d), state


def kernel(x_prompt, x_sample, mem_prompt, cache_a_k, cache_a_v, cache_b_k, cache_b_v, cache_mem_k,
           cache_mem_v, norm_g, w_ffn_gu, w_ffn_dn, w_in, rel_bias, diff_lambda, subln_g, w_out,
           mem_norm_g, w_cq, w_ckv, w_co):
    depth = norm_g.shape[0]
    b, t, d = x_prompt.shape
    bd, sd, _ = x_sample.shape
    n_mem = mem_prompt.shape[1]
    past_a, past_b = cache_a_k.shape[2], cache_b_k.shape[2]
    width = w_in.shape[2] // 6

    cache = (jnp.transpose(cache_a_k, (0, 1, 3, 4, 2)).reshape(depth, bd, HA * DHA, past_a),
             jnp.transpose(cache_a_v, (0, 1, 3, 4, 2)).reshape(depth, bd, HA * DHA, past_a),
             jnp.transpose(cache_b_k, (0, 1, 3, 4, 5, 2)).reshape(depth, bd, HB, 2 * DHB, past_b),
             cache_b_v.reshape(depth, bd, past_b * HB, 2 * DHB))
    smem_k = cache_mem_k.reshape(depth, bd * n_mem * HC, DHC)
    smem_v = cache_mem_v.reshape(depth, bd * n_mem * HC, DHC)

    weights = dict(
        w_gu=w_ffn_gu.astype(BF16), w_dn=w_ffn_dn.astype(BF16), w_in=w_in.astype(BF16),
        w_kbt=jnp.transpose(w_in[:, :, 4 * width:5 * width], (0, 2, 1)).astype(BF16),
        w_at=jnp.transpose(w_in[:, :, width:3 * width].reshape(depth, d, 2, width),
                           (0, 2, 3, 1)).astype(BF16),
        w_out=w_out.astype(BF16), w_cq=w_cq.astype(BF16), w_co=w_co.astype(BF16))
    w_ckv = w_ckv.astype(BF16)

    xp, xs = x_prompt, x_sample
    s_state = []
    pmem = carried = None
    for l in range(depth):
        lam_init = 0.8 - 0.6 * math.exp(-0.3 * l)
        lw = dict(weights, norm_g=norm_g[l], bias=_band_bias(rel_bias[l]),
                  consts=jnp.asarray([lam_init], F32), diff_lambda=diff_lambda[l],
                  subln_g=subln_g[l].reshape(1, -1))
        pmem = _memory_kv(l, depth, mem_prompt.reshape(b * n_mem, d), mem_norm_g[l].reshape(1, d),
                          w_ckv, pmem)
        xp, st_p = _trunk_layer(l, depth, xp, pmem[0], pmem[1], None, lw, carried)
        xs, st_s = _trunk_layer(l, depth, xs, smem_k, smem_v, cache, lw)
        carried = st_p
        s_state.append(st_s)

    def stack(items, idx, shape):
        return jnp.stack([it[idx] for it in items]).reshape((depth,) + shape)

    keep = min(WINDOW_A, t)
    kt_all, v_all, kat_all, vat_all = carried
    return (xp, xs,
            jnp.transpose(kat_all.reshape(depth, b, HA, DHA, keep), (0, 1, 4, 2, 3)),
            jnp.transpose(vat_all.reshape(depth, b, HA, DHA, keep), (0, 1, 4, 2, 3)),
            jnp.transpose(kt_all.reshape(depth, b, HB, 2, DHB, t), (0, 1, 5, 2, 3, 4)),
            v_all.reshape(depth, b, t, HB, 2 * DHB),
            pmem[0].reshape(depth, b, n_mem, HC, DHC), pmem[1].reshape(depth, b, n_mem, HC, DHC),
            stack(s_state, 0, (bd, sd, HA, DHA)), stack(s_state, 1, (bd, sd, HA, DHA)),
            stack(s_state, 2, (bd, sd, HB, 2, DHB)), stack(s_state, 3, (bd, sd, HB, 2 * DHB)))
```

```python
import functools
import math

import jax
import jax.numpy as jnp
import numpy as np
from jax import lax
from jax.experimental import pallas as pl
from jax.experimental.pallas import tpu as pltpu

F32 = jnp.float32
BF16 = jnp.bfloat16

CHUNK = 64
BAND_CHUNKS = 8
WINDOW_A = BAND_CHUNKS * CHUNK
BAND = WINDOW_A + CHUNK
HA, DHA = 8, 64
HB, DHB = 4, 64
HC, DHC = 4, 128
REL_CLIP = 128
EPS = 1e-6
NEG_INF = -1e30
LOG2E = math.log2(math.e)
ALIBI_SLOPES = tuple(2.0 ** (-8.0 * (i + 1) / HB) for i in range(HB))

LANES = 128
VMEM_LIMIT = 56 * 1024 * 1024
ROW_TILE = 1024
PROJ_TILE = 512
FF_CHUNK = 256
MIX_BATCHES = 8
BAND_UNROLL = 2
DIFF_TQ = 256
DIFF_TK = 256
ONES_ROWS = 16
N_SLOPE_TERMS = 4


def _params(n_axes):
    return pltpu.CompilerParams(
        dimension_semantics=("arbitrary",) * n_axes, vmem_limit_bytes=VMEM_LIMIT)


def _rms(x, g):
    ms = jnp.mean(x * x, axis=-1, keepdims=True)
    return x * lax.rsqrt(ms + EPS) * g


def _dot(a, b):
    return jnp.dot(a, b, preferred_element_type=F32)


def _dot_nt(a, b):
    return lax.dot_general(a, b, (((1,), (1,)), ((), ())), preferred_element_type=F32)


def _chunk_of(pos):
    assert CHUNK & (CHUNK - 1) == 0
    return lax.shift_right_logical(pos, jnp.int32(CHUNK.bit_length() - 1))


def _resident(shape):
    zeros = (0,) * len(shape)
    return pl.BlockSpec(shape, lambda *_: zeros, pipeline_mode=pl.Buffered(1))


def _weight(w, lead):
    index = tuple(lead) + (0,) * (w.ndim - len(lead))
    return pl.BlockSpec((None,) * len(lead) + w.shape[len(lead):], lambda *_: index,
                        pipeline_mode=pl.Buffered(1))


def _ffn_kernel(x_ref, gpre_ref, gpost_ref, wgu_ref, wdn_ref, o_ref, acc_ref, *, d_ff):
    half = x_ref.shape[0] // 2
    rows = [slice(0, half), slice(half, 2 * half)]
    xs = [x_ref[r, :] for r in rows]
    xn = [_rms(x, gpre_ref[...]).astype(BF16) for x in xs]
    for i, r in enumerate(rows):
        for c in range(d_ff // FF_CHUNK):
            lo = c * FF_CHUNK
            gate = _dot(xn[i], wgu_ref[:, lo:lo + FF_CHUNK])
            up = _dot(xn[i], wgu_ref[:, d_ff + lo:d_ff + lo + FF_CHUNK])
            hidden = (gate * jax.nn.sigmoid(gate) * up).astype(BF16)
            part = _dot(hidden, wdn_ref[lo:lo + FF_CHUNK, :])
            if c == 0:
                acc_ref[r, :] = part
            else:
                acc_ref[r, :] += part
    for i, r in enumerate(rows):
        o_ref[r, :] = xs[i] + 0.5 * _rms(acc_ref[r, :], gpost_ref[...])


def _ffn_half(x, g_pre, g_post, w_gu, w_dn, lead):
    m, d = x.shape
    d_ff = w_dn.shape[-2]
    assert d_ff % FF_CHUNK == 0 and m % ROW_TILE == 0
    row = pl.BlockSpec((ROW_TILE, d), lambda i: (i, 0))
    return pl.pallas_call(
        functools.partial(_ffn_kernel, d_ff=d_ff),
        grid=(m // ROW_TILE,),
        in_specs=[row, _resident((1, d)), _resident((1, d)), _weight(w_gu, lead), _weight(w_dn, lead)],
        out_specs=row,
        out_shape=jax.ShapeDtypeStruct((m, d), F32),
        scratch_shapes=[pltpu.VMEM((ROW_TILE, d), F32)],
        compiler_params=_params(1),
        name="ffn_half",
    )(x, g_pre, g_post, w_gu, w_dn)


def _store_head_rows(o_ref, lead, y):
    m, heads = y.shape[0], y.shape[1] // LANES
    for h in range(heads):
        o_ref[lead, pl.ds(h, m, stride=heads), :] = y[:, h * LANES:(h + 1) * LANES]


def _zero_other_layers(o_ref, lead):
    for l in range(o_ref.shape[0]):
        if l != lead:
            o_ref[l] = jnp.zeros(o_ref.shape[1:], o_ref.dtype)


def _proj_kernel(*refs, width, state_layout, n_alias, per, keep, lead):
    n_in = 4 + (1 if state_layout else 0)
    x_ref, g_ref, w_ref, wkt_ref = refs[:4]
    outs = refs[n_in + n_alias:]
    qa_ref, ka_ref, va_ref, qb_ref, kb_ref, vb_ref = outs[:6]
    u = _rms(x_ref[...], g_ref[...]).astype(BF16)
    qa_ref[...] = _dot(u, w_ref[:, 0:width]) * (DHA ** -0.5 * LOG2E)
    ka_ref[...] = _dot(u, w_ref[:, width:2 * width])
    va_ref[...] = _dot(u, w_ref[:, 2 * width:3 * width])
    qb_ref[...] = _dot(u, w_ref[:, 3 * width:4 * width]) * (DHB ** -0.5 * LOG2E)
    vb = _dot(u, w_ref[:, 5 * width:6 * width])
    if state_layout:
        wat_ref = refs[4]
        kat_ref, vat_ref = outs[6:]
        _zero_other_layers(kb_ref, lead)
        _zero_other_layers(vb_ref, lead)
        kb_ref[lead, 0] = _dot_nt(wkt_ref[...], u)
        _store_head_rows(vb_ref, lead, vb)

        @pl.when(pl.program_id(0) % per == per - 1)
        def _():
            tail = u[u.shape[0] - keep:, :]
            _zero_other_layers(kat_ref, lead)
            _zero_other_layers(vat_ref, lead)
            kat_ref[lead, 0] = _dot_nt(wat_ref[0], tail)
            vat_ref[lead, 0] = _dot_nt(wat_ref[1], tail)
    else:
        kb_ref[...] = _dot(u, w_ref[:, 4 * width:5 * width])
        vb_ref[...] = vb


def _in_proj(layer, x, g, w_in, w_kbt, w_at=None, state=None):
    m, d = x.shape
    width = w_in.shape[-1] // 6
    row = pl.BlockSpec((PROJ_TILE, d), lambda i: (i, 0))
    out = pl.BlockSpec((PROJ_TILE, width), lambda i: (i, 0))
    f32_rows = jax.ShapeDtypeStruct((m, width), F32)
    in_specs = [row, _resident((1, d)), _weight(w_in, (layer,)), _weight(w_kbt, (layer,))]
    args = [x, g, w_in, w_kbt]
    aliases = {}
    per = keep = lead = 0
    if state is None:
        out_specs, out_shape = [out] * 6, [f32_rows] * 6
    else:
        depth, batch, buffers = state
        t = m // batch
        per = t // PROJ_TILE
        keep = min(WINDOW_A, t)
        assert t % PROJ_TILE == 0 and keep <= PROJ_TILE
        in_specs.append(_weight(w_at, (layer,)))
        args.append(w_at)
        n_lead, at = (depth, 0) if buffers is None else (1, layer)
        lead = layer if buffers is None else 0
        kt_spec = pl.BlockSpec((n_lead, 1, width, PROJ_TILE), lambda i: (at, i // per, 0, i % per))
        v_spec = pl.BlockSpec((n_lead, PROJ_TILE * HB, LANES), lambda i: (at, i, 0))
        tail_spec = pl.BlockSpec((n_lead, 1, width, keep), lambda i: (at, i // per, 0, 0))
        tail_shape = jax.ShapeDtypeStruct((depth, batch, width, keep), F32)
        out_specs = [out] * 4 + [kt_spec, v_spec, tail_spec, tail_spec]
        out_shape = [f32_rows] * 4 + [jax.ShapeDtypeStruct((depth, batch, width, t), F32),
                                      jax.ShapeDtypeStruct((depth, m * HB, LANES), F32),
                                      tail_shape, tail_shape]
        if buffers is not None:
            in_specs += [pl.BlockSpec(memory_space=pl.ANY)] * 4
            args += list(buffers)
            aliases = {5 + n: 4 + n for n in range(4)}
    return pl.pallas_call(
        functools.partial(_proj_kernel, width=width, state_layout=state is not None,
                          n_alias=len(aliases), per=per, keep=keep, lead=lead),
        grid=(m // PROJ_TILE,),
        in_specs=in_specs,
        out_specs=out_specs,
        out_shape=out_shape,
        input_output_aliases=aliases,
        compiler_params=_params(1),
        name="in_proj",
    )(*args)


def _memkv_kernel(*refs, width, n_alias, lead):
    x_ref, g_ref, w_ref = refs[:3]
    k_ref, v_ref = refs[3 + n_alias:]
    u = _rms(x_ref[...], g_ref[...]).astype(BF16)
    _zero_other_layers(k_ref, lead)
    _zero_other_layers(v_ref, lead)
    _store_head_rows(k_ref, lead, _dot(u, w_ref[:, :width]))
    _store_head_rows(v_ref, lead, _dot(u, w_ref[:, width:]))


def _memory_kv(layer, depth, mem, g, w_ckv, prev=None):
    m, d = mem.shape
    width = w_ckv.shape[-1] // 2
    row = pl.BlockSpec((ROW_TILE, d), lambda i: (i, 0))
    n_lead, at, lead = (depth, 0, layer) if prev is None else (1, layer, 0)
    out = pl.BlockSpec((n_lead, ROW_TILE * HC, LANES), lambda i: (at, i, 0))
    in_specs = [row, _resident((1, d)), _weight(w_ckv, (layer,))]
    args = [mem, g, w_ckv]
    aliases = {}
    if prev is not None:
        in_specs += [pl.BlockSpec(memory_space=pl.ANY)] * 2
        args += list(prev)
        aliases = {3: 0, 4: 1}
    return pl.pallas_call(
        functools.partial(_memkv_kernel, width=width, n_alias=len(aliases), lead=lead),
        grid=(m // ROW_TILE,),
        in_specs=in_specs,
        out_specs=[out] * 2,
        out_shape=[jax.ShapeDtypeStruct((depth, m * HC, LANES), F32)] * 2,
        input_output_aliases=aliases,
        compiler_params=_params(1),
        name="memory_kv",
    )(*args)


def _bias_kernel(table_ref, o_ref):
    pair = pl.program_id(0)
    row = lax.broadcasted_iota(jnp.int32, (2 * CHUNK, BAND), 0)
    j = lax.broadcasted_iota(jnp.int32, (2 * CHUNK, BAND), 1)
    odd = row >= CHUNK
    idx = jnp.clip((row & (CHUNK - 1)) - (j - WINDOW_A), -REL_CLIP, REL_CLIP) + REL_CLIP

    def pick(r, acc):
        return jnp.where(idx == r, jnp.where(odd, table_ref[2 * pair + 1, r], table_ref[2 * pair, r]), acc)

    first = max(0, REL_CLIP - CHUNK + 1)
    o_ref[0] = lax.fori_loop(first, 2 * REL_CLIP + 1, pick, jnp.zeros((2 * CHUNK, BAND), F32)) * LOG2E


def _band_bias(table):
    return pl.pallas_call(
        _bias_kernel,
        grid=(HA // 2,),
        in_specs=[pl.BlockSpec(memory_space=pltpu.SMEM)],
        out_specs=pl.BlockSpec((1, 2 * CHUNK, BAND), lambda p: (p, 0, 0)),
        out_shape=jax.ShapeDtypeStruct((HA // 2, 2 * CHUNK, BAND), F32),
        compiler_params=_params(1),
        name="band_bias",
    )(table)


def _split_heads(x):
    lane = lax.broadcasted_iota(jnp.int32, x.shape, 1)
    lo = jnp.where(lane < LANES // 2, x, 0.0).astype(BF16)
    hi = jnp.where(lane >= LANES // 2, x, 0.0).astype(BF16)
    return lo, hi


def _stack_heads(x):
    return jnp.concatenate(_split_heads(x), axis=0)


def _unstack_heads(o):
    m = o.shape[0] // 2
    lane = lax.broadcasted_iota(jnp.int32, (m, LANES), 1)
    return jnp.where(lane < LANES // 2, o[:m], o[m:])


def _exp2_rows(s):
    p = jnp.exp2(s - jnp.max(s, axis=-1, keepdims=True))
    return p.astype(BF16), jnp.sum(p, axis=-1, keepdims=True)


def _band_prompt_kernel(q_ref, kp_ref, kc_ref, vp_ref, vc_ref, bias_ref, o_ref, kcat_ref, vcat_ref):
    t = pl.program_id(1)
    tile = kc_ref.shape[1]
    kcat_ref[0:tile, :] = kp_ref[0].astype(BF16)
    kcat_ref[tile:2 * tile, :] = kc_ref[0].astype(BF16)
    vcat_ref[0:tile, :] = vp_ref[0].astype(BF16)
    vcat_ref[tile:2 * tile, :] = vc_ref[0].astype(BF16)
    col = lax.broadcasted_iota(jnp.int32, (2 * CHUNK, BAND), 1)

    def step(i, carry):
        work = []
        for sub in range(BAND_UNROLL):
            r0 = pl.multiple_of((i * BAND_UNROLL + sub) * CHUNK, CHUNK)
            valid = (col >= WINDOW_A - r0) | (t > 0)
            for pair in range(HA // 2):
                lanes = slice(pair * LANES, (pair + 1) * LANES)
                q2 = _stack_heads(q_ref[0, pl.ds(r0, CHUNK), lanes])
                work.append((r0, pair, lanes, valid, _dot_nt(q2, kcat_ref[pl.ds(r0, BAND), lanes])))
        probs = []
        for r0, pair, lanes, valid, s in work:
            probs.append((r0, lanes) + _exp2_rows(jnp.where(valid, s + bias_ref[pair], NEG_INF)))
        for r0, lanes, p, l in probs:
            o = _unstack_heads(_dot(p, vcat_ref[pl.ds(r0, BAND), lanes]) / l)
            o_ref[0, pl.ds(r0, CHUNK), lanes] = o.astype(o_ref.dtype)
        return carry

    lax.fori_loop(0, tile // (CHUNK * BAND_UNROLL), step, 0)


def _band_prompt(q, k, v, bias):
    b, t, w = q.shape
    tile = WINDOW_A
    assert t % tile == 0
    cur = pl.BlockSpec((1, tile, w), lambda i, j: (i, j, 0))
    prev = pl.BlockSpec((1, tile, w), lambda i, j: (i, jnp.maximum(j - 1, 0), 0))
    return pl.pallas_call(
        _band_prompt_kernel,
        grid=(b, t // tile),
        in_specs=[cur, prev, cur, prev, cur, _resident(bias.shape)],
        out_specs=cur,
        out_shape=jax.ShapeDtypeStruct((b, t, w), BF16),
        scratch_shapes=[pltpu.VMEM((2 * tile, w), BF16), pltpu.VMEM((2 * tile, w), BF16)],
        compiler_params=_params(2),
        name="band_prompt",
    )(q, k, k, v, v, bias)


def _band_sample_kernel(q_ref, kt_ref, kn_ref, vt_ref, vn_ref, bias_ref, o_ref):
    past = kt_ref.shape[3]
    s_len = q_ref.shape[1]
    work = []
    for pair in range(HA // 2):
        lanes = slice(pair * LANES, (pair + 1) * LANES)
        q2 = _stack_heads(q_ref[0, :, lanes])
        s_old = _dot(q2, kt_ref[0, 0, lanes, :].astype(BF16))
        s_new = _dot_nt(q2, kn_ref[0, :, lanes].astype(BF16))
        work.append((pair, lanes, s_old, s_new))
    probs = []
    for pair, lanes, s_old, s_new in work:
        bias = jnp.concatenate([bias_ref[pair, 0:s_len, :], bias_ref[pair, CHUNK:CHUNK + s_len, :]],
                               axis=0)
        s_old = s_old + bias[:, 0:past]
        s_new = s_new + bias[:, past:past + s_len]
        m = jnp.maximum(jnp.max(s_old, axis=-1, keepdims=True), jnp.max(s_new, axis=-1, keepdims=True))
        p_old = jnp.exp2(s_old - m)
        p_new = jnp.exp2(s_new - m)
        l = jnp.sum(p_old, axis=-1, keepdims=True) + jnp.sum(p_new, axis=-1, keepdims=True)
        probs.append((lanes, p_old.astype(BF16), p_new.astype(BF16), l))
    for lanes, p_old, p_new, l in probs:
        o = _dot_nt(p_old, vt_ref[0, 0, lanes, :].astype(BF16)) + _dot(p_new, vn_ref[0, :, lanes].astype(BF16))
        o_ref[0, :, lanes] = _unstack_heads(o / l).astype(o_ref.dtype)


def _band_sample(layer, q, k, v, cache_kt, cache_vt, bias):
    b, s_len, w = q.shape
    past = cache_kt.shape[3]
    assert past == WINDOW_A and s_len <= CHUNK
    new = pl.BlockSpec((1, s_len, w), lambda i: (i, 0, 0))
    old = pl.BlockSpec((1, 1, w, past), lambda i: (layer, i, 0, 0))
    return pl.pallas_call(
        _band_sample_kernel,
        grid=(b,),
        in_specs=[new, old, new, old, new, _resident(bias.shape)],
        out_specs=new,
        out_shape=jax.ShapeDtypeStruct((b, s_len, w), BF16),
        compiler_params=_params(1),
        name="band_sample",
    )(q, cache_kt, k, cache_vt, v, bias)


def _diff_lambda(dl_ref, lam_init):
    lp = dl_ref[...]
    a = jnp.sum(lp[0:1] * lp[1:2], axis=-1, keepdims=True)
    b = jnp.sum(lp[2:3] * lp[3:4], axis=-1, keepdims=True)
    return jnp.exp(a) - jnp.exp(b) + lam_init


def _bf16_terms(x):
    terms, rest = [], float(x)
    for _ in range(N_SLOPE_TERMS):
        term = float(np.asarray(rest, np.float32).astype(BF16))
        terms.append(term)
        rest -= term
    return terms


def _key_position_features(kpos, lane):
    hi = lax.shift_left(_chunk_of(kpos), jnp.int32(CHUNK.bit_length() - 1))
    lo = kpos & (CHUNK - 1)
    feat = jnp.where((lane & 1) == 0, hi, lo)
    return jnp.where(lane < 2 * N_SLOPE_TERMS, feat, 0).astype(F32).astype(BF16)


def _slope_features(head, lane):
    feat = jnp.zeros(lane.shape, F32)
    for i, term in enumerate(_bf16_terms(ALIBI_SLOPES[head] * LOG2E)):
        feat = jnp.where(lax.shift_right_logical(lane, jnp.int32(1)) == i, term, feat)
    return feat.astype(BF16)


def _diff_prompt_kernel(consts_ref, dl_ref, sg_ref, q_ref, kt_ref, v_ref, o_ref,
                        kaug_ref, vt_ref, acc_ref):
    qi = pl.program_id(1)
    tq, tk = DIFF_TQ, DIFF_TK
    t = kt_ref.shape[3]
    lam_init = consts_ref[0]
    lam = _diff_lambda(dl_ref, lam_init)

    @pl.when(qi == 0)
    def _():
        lane = lax.broadcasted_iota(jnp.int32, (tk, LANES), 1)
        row = lax.broadcasted_iota(jnp.int32, (tk, LANES), 0)

        def build(blk, carry):
            r0 = pl.multiple_of(blk * tk, tk)
            feat = _key_position_features(r0 + row, lane)
            for h in range(HB):
                k = kt_ref[0, 0, h * LANES:(h + 1) * LANES, pl.ds(r0, tk)]
                v = v_ref[0, pl.ds(r0 * HB + h, tk, stride=HB), :]
                kaug_ref[h, pl.ds(r0, tk), 0:LANES] = k.T.astype(BF16)
                kaug_ref[h, pl.ds(r0, tk), LANES:2 * LANES] = feat
                vt_ref[h, blk, 0:LANES, :] = v.T.astype(BF16)
                vt_ref[h, blk, LANES:LANES + ONES_ROWS, :] = jnp.ones((ONES_ROWS, tk), BF16)
            return carry

        lax.fori_loop(0, t // tk, build, 0)

    q = q_ref[0]
    qlane = lax.broadcasted_iota(jnp.int32, (tq, LANES), 1)
    qd = []
    for h in range(HB):
        q_lo, q_hi = _split_heads(q[:, h * LANES:(h + 1) * LANES])
        feat = _slope_features(h, qlane)
        qd.append(jnp.concatenate([jnp.concatenate([q_lo, feat], axis=1),
                                   jnp.concatenate([q_hi, feat], axis=1)], axis=0))
        acc_ref[h] = jnp.zeros(acc_ref.shape[1:], F32)

    def scores(j):
        off = pl.multiple_of(j * tk, tk)
        return [_dot_nt(kaug_ref[h, pl.ds(off, tk), :], qd[h]) for h in range(HB)]

    def update(j, ss, ms):
        out = []
        for h in range(HB):
            m_new = jnp.maximum(ms[h], jnp.max(ss[h], axis=0, keepdims=True))
            p = jnp.exp2(ss[h] - m_new).astype(BF16)
            acc_ref[h] = jnp.exp2(ms[h] - m_new) * acc_ref[h] + _dot(vt_ref[h, j], p)
            out.append(m_new)
        return tuple(out)

    def blocks(n):
        def body(jj, ms):
            ss = [scores(n * jj + i) for i in range(n)]
            for i in range(n):
                ms = update(n * jj + i, ss[i], ms)
            return ms
        return body

    jd = (qi * tq) // tk
    ms = tuple(jnp.full((1, 2 * tq), NEG_INF, F32) for _ in range(HB))
    ms = lax.fori_loop(0, jd // 4, blocks(4), ms)
    ms = lax.cond(jd % 4 >= 2, lambda c: blocks(2)(2 * (jd // 4), c), lambda c: c, ms)
    ms = lax.cond(jd % 2 == 1, lambda c: blocks(1)(jd - 1, c), lambda c: c, ms)

    kpos = jd * tk + lax.broadcasted_iota(jnp.int32, (tk, 2 * tq), 0)
    qpos = qi * tq + (lax.broadcasted_iota(jnp.int32, (tk, 2 * tq), 1) & (tq - 1))
    ahead = jnp.maximum(kpos - qpos, 0).astype(F32)
    visible = _chunk_of(kpos) <= _chunk_of(qpos)
    ss = [jnp.where(visible, s - (2.0 * ALIBI_SLOPES[h] * LOG2E) * ahead, NEG_INF)
          for h, s in enumerate(scores(jd))]
    update(jd, ss, ms)
    for h in range(HB):
        acc = acc_ref[h, 0:LANES, :] / acc_ref[h, LANES:LANES + 1, :]
        o = (acc[:, :tq] - lam * acc[:, tq:]).T
        o = _rms(o, sg_ref[...]) * (1.0 - lam_init)
        o_ref[0, :, h * LANES:(h + 1) * LANES] = o.astype(o_ref.dtype)


def _diff_prompt(layer, q, kt_all, v_all, consts, diff_lambda, subln_g):
    b, t, w = q.shape
    tq, tk = DIFF_TQ, DIFF_TK
    assert t % tk == 0 and tk == tq and tq % CHUNK == 0 and t < 2 ** 14
    assert tq & (tq - 1) == 0
    smem = pl.BlockSpec(memory_space=pltpu.SMEM)
    qspec = pl.BlockSpec((1, tq, w), lambda i, j: (i, j, 0))
    kspec = pl.BlockSpec((1, 1, w, t), lambda i, j: (layer, i, 0, 0), pipeline_mode=pl.Buffered(1))
    vspec = pl.BlockSpec((1, t * HB, LANES), lambda i, j: (layer, i, 0), pipeline_mode=pl.Buffered(1))
    return pl.pallas_call(
        _diff_prompt_kernel,
        grid=(b, t // tq),
        in_specs=[smem, _resident(diff_lambda.shape), _resident(subln_g.shape), qspec, kspec, vspec],
        out_specs=qspec,
        out_shape=jax.ShapeDtypeStruct((b, t, w), BF16),
        scratch_shapes=[pltpu.VMEM((HB, t, 2 * LANES), BF16),
                        pltpu.VMEM((HB, t // tk, LANES + ONES_ROWS, tk), BF16),
                        pltpu.VMEM((HB, LANES + ONES_ROWS, 2 * tq), F32)],
        compiler_params=_params(2),
        name="diff_prompt",
    )(consts, diff_lambda, subln_g, q, kt_all, v_all)


def _diff_sample_kernel(consts_ref, dl_ref, sg_ref, q_ref, kt_ref, kn_ref, v_ref, vn_ref, o_ref):
    past = kt_ref.shape[4]
    s_len = q_ref.shape[1]
    lam_init = consts_ref[0]
    lam = _diff_lambda(dl_ref, lam_init)
    kpos = lax.broadcasted_iota(jnp.int32, (1, past), 1).astype(F32)
    row = lax.broadcasted_iota(jnp.int32, (2 * s_len, 1), 0)
    qloc = jnp.where(row >= s_len, row - s_len, row)
    kcol = lax.broadcasted_iota(jnp.int32, (1, s_len), 1)
    near = (past + qloc - jnp.abs(qloc - kcol)).astype(F32)
    for h in range(HB):
        lanes = slice(h * LANES, (h + 1) * LANES)
        c = ALIBI_SLOPES[h] * LOG2E
        q_lo, q_hi = _split_heads(q_ref[0, :, lanes])
        q2 = jnp.concatenate([q_lo, q_hi], axis=0)
        s_old = _dot(q2, kt_ref[0, 0, h].astype(BF16)) + c * kpos
        s_new = _dot_nt(q2, kn_ref[0, :, lanes].astype(BF16)) + c * near
        m = jnp.maximum(jnp.max(s_old, axis=-1, keepdims=True),
                        jnp.max(s_new, axis=-1, keepdims=True))
        p_old = jnp.exp2(s_old - m)
        p_new = jnp.exp2(s_new - m)
        l = jnp.sum(p_old, axis=-1, keepdims=True) + jnp.sum(p_new, axis=-1, keepdims=True)
        v_old = v_ref[0, 0, pl.ds(h, past, stride=HB), :].astype(BF16)
        acc = _dot(p_old.astype(BF16), v_old) + _dot(p_new.astype(BF16), vn_ref[0, :, lanes].astype(BF16))
        o = acc / l
        o = o[:s_len] - lam * o[s_len:]
        o = _rms(o, sg_ref[...]) * (1.0 - lam_init)
        o_ref[0, :, lanes] = o.astype(o_ref.dtype)


def _diff_sample(layer, q, k, v, cache_kt, cache_v, consts, diff_lambda, subln_g):
    b, s_len, w = q.shape
    past = cache_kt.shape[4]
    assert past % CHUNK + s_len <= CHUNK
    smem = pl.BlockSpec(memory_space=pltpu.SMEM)
    new = pl.BlockSpec((1, s_len, w), lambda i: (i, 0, 0))
    old_k = pl.BlockSpec((1, 1, HB, LANES, past), lambda i: (layer, i, 0, 0, 0))
    old_v = pl.BlockSpec((1, 1, past * HB, LANES), lambda i: (layer, i, 0, 0))
    return pl.pallas_call(
        _diff_sample_kernel,
        grid=(b,),
        in_specs=[smem, _resident(diff_lambda.shape), _resident(subln_g.shape),
                  new, old_k, new, old_v, new],
        out_specs=new,
        out_shape=jax.ShapeDtypeStruct((b, s_len, w), BF16),
        compiler_params=_params(1),
        name="diff_sample",
    )(consts, diff_lambda, subln_g, q, cache_kt, k, cache_v, v)


def _softmax_pv(s, v):
    m = jnp.max(s, axis=-1, keepdims=True)
    p = jnp.exp(s - m)
    l = jnp.sum(p, axis=-1, keepdims=True)
    return _dot(p.astype(BF16), v) / l


def _mix_kernel(x_ref, oa_ref, ob_ref, mk_ref, mv_ref, g_ref, wout_ref, wcq_ref, wco_ref, o_ref,
                *, n_batch):
    x = x_ref[...]
    wa = oa_ref.shape[1]
    rows_b = x.shape[0] // n_batch
    rows_mem = mk_ref.shape[1] // n_batch
    n_mem = rows_mem // HC
    mix = _dot(oa_ref[...], wout_ref[:wa, :]) + _dot(ob_ref[...], wout_ref[wa:, :])
    x = x + _rms(mix, g_ref[0:1])
    qc = _dot(_rms(x, g_ref[1:2]).astype(BF16), wcq_ref[...]).astype(BF16)
    outs = []
    for bi in range(n_batch):
        heads = []
        for h in range(HC):
            q = qc[bi * rows_b:(bi + 1) * rows_b, h * DHC:(h + 1) * DHC]
            mk = mk_ref[0, pl.ds(bi * rows_mem + h, n_mem, stride=HC), :].astype(BF16)
            mv = mv_ref[0, pl.ds(bi * rows_mem + h, n_mem, stride=HC), :].astype(BF16)
            heads.append(_softmax_pv(_dot_nt(q, mk) * DHC ** -0.5, mv).astype(BF16))
        outs.append(jnp.concatenate(heads, axis=-1))
    oc = _dot(outs[0] if n_batch == 1 else jnp.concatenate(outs, axis=0), wco_ref[...])
    o_ref[...] = x + _rms(oc, g_ref[2:3])


def _mix_cross(layer, x, oa, ob, mem_k, mem_v, g345, w_out, w_cq, w_co, batch):
    m, d = x.shape
    t = m // batch
    n_batch = 1 if t >= ROW_TILE else MIX_BATCHES
    tm = ROW_TILE if n_batch == 1 else n_batch * t
    assert m % tm == 0 and (t % tm == 0 or n_batch > 1)
    per = max(1, t // tm)
    rows_mem = mem_k.shape[1] // batch

    def rows(width):
        return pl.BlockSpec((tm, width), lambda i: (i, 0))

    mem = pl.BlockSpec((1, n_batch * rows_mem, LANES), lambda i: (layer, i // per, 0))
    return pl.pallas_call(
        functools.partial(_mix_kernel, n_batch=n_batch),
        grid=(m // tm,),
        in_specs=[rows(d), rows(oa.shape[1]), rows(ob.shape[1]), mem, mem, _resident(g345.shape),
                  _weight(w_out, (layer,)), _weight(w_cq, (layer,)), _weight(w_co, (layer,))],
        out_specs=rows(d),
        out_shape=jax.ShapeDtypeStruct((m, d), F32),
        compiler_params=_params(1),
        name="mix_cross",
    )(x, oa, ob, mem_k, mem_v, g345, w_out, w_cq, w_co)


def _trunk_layer(layer, depth, x, mem_k, mem_v, cache, lw, carried=None):
    b, t, d = x.shape
    g = lw["norm_g"]
    x2 = _ffn_half(x.reshape(b * t, d), g[0:1], g[1:2], lw["w_gu"], lw["w_dn"], (layer, 0))
    if cache is None:
        qa, ka, va, qb, *state = _in_proj(layer, x2, g[2:3], lw["w_in"], lw["w_kbt"], lw["w_at"],
                                          (depth, b, carried))
        qa, ka, va, qb = (y.reshape(b, t, -1) for y in (qa, ka, va, qb))
        oa = _band_prompt(qa, ka, va, lw["bias"])
        ob = _diff_prompt(layer, qb, state[0], state[1], lw["consts"], lw["diff_lambda"],
                          lw["subln_g"])
    else:
        qa, ka, va, qb, kb, vb = (y.reshape(b, t, -1)
                                  for y in _in_proj(layer, x2, g[2:3], lw["w_in"], lw["w_kbt"]))
        ca_kt, ca_vt, cb_kt, cb_v = cache
        oa = _band_sample(layer, qa, ka, va, ca_kt, ca_vt, lw["bias"])
        ob = _diff_sample(layer, qb, kb, vb, cb_kt, cb_v, lw["consts"], lw["diff_lambda"],
                          lw["subln_g"])
        state = (ka, va, kb, vb)
    x3 = _mix_cross(layer, x2, oa.reshape(b * t, -1), ob.reshape(b * t, -1), mem_k, mem_v, g[3:6],
                    lw["w_out"], lw["w_cq"], lw["w_co"], b)
    x4 = _ffn_half(x3, g[6:7], g[7:8], lw["w_gu"], lw["w_dn"], (layer, 1))
    return x4.reshape(b, t, d), state


def kernel(x_prompt, x_sample, mem_prompt, cache_a_k, cache_a_v, cache_b_k, cache_b_v, cache_mem_k,
           cache_mem_v, norm_g, w_ffn_gu, w_ffn_dn, w_in, rel_bias, diff_lambda, subln_g, w_out,
           mem_norm_g, w_cq, w_ckv, w_co):
    depth = norm_g.shape[0]
    b, t, d = x_prompt.shape
    bd, sd, _ = x_sample.shape
    n_mem = mem_prompt.shape[1]
    past_a, past_b = cache_a_k.shape[2], cache_b_k.shape[2]
    width = w_in.shape[2] // 6

    cache = (jnp.transpose(cache_a_k, (0, 1, 3, 4, 2)).reshape(depth, bd, HA * DHA, past_a),
             jnp.transpose(cache_a_v, (0, 1, 3, 4, 2)).reshape(depth, bd, HA * DHA, past_a),
             jnp.transpose(cache_b_k, (0, 1, 3, 4, 5, 2)).reshape(depth, bd, HB, 2 * DHB, past_b),
             cache_b_v.reshape(depth, bd, past_b * HB, 2 * DHB))
    smem_k = cache_mem_k.reshape(depth, bd * n_mem * HC, DHC)
    smem_v = cache_mem_v.reshape(depth, bd * n_mem * HC, DHC)

    weights = dict(
        w_gu=w_ffn_gu.astype(BF16), w_dn=w_ffn_dn.astype(BF16), w_in=w_in.astype(BF16),
        w_kbt=jnp.transpose(w_in[:, :, 4 * width:5 * width], (0, 2, 1)).astype(BF16),
        w_at=jnp.transpose(w_in[:, :, width:3 * width].reshape(depth, d, 2, width),
                           (0, 2, 3, 1)).astype(BF16),
        w_out=w_out.astype(BF16), w_cq=w_cq.astype(BF16), w_co=w_co.astype(BF16))
    w_ckv = w_ckv.astype(BF16)

    xp, xs = x_prompt, x_sample
    s_state = []
    pmem = carried = None
    for l in range(depth):
        lam_init = 0.8 - 0.6 * math.exp(-0.3 * l)
        lw = dict(weights, norm_g=norm_g[l], bias=_band_bias(rel_bias[l]),
                  consts=jnp.asarray([lam_init], F32), diff_lambda=diff_lambda[l],
                  subln_g=subln_g[l].reshape(1, -1))
        pmem = _memory_kv(l, depth, mem_prompt.reshape(b * n_mem, d), mem_norm_g[l].reshape(1, d),
                          w_ckv, pmem)
        xp, st_p = _trunk_layer(l, depth, xp, pmem[0], pmem[1], None, lw, carried)
        xs, st_s = _trunk_layer(l, depth, xs, smem_k, smem_v, cache, lw)
        carried = st_p
        s_state.append(st_s)

    def stack(items, idx, shape):
        return jnp.stack([it[idx] for it in items]).reshape((depth,) + shape)

    keep = min(WINDOW_A, t)
    kt_all, v_all, kat_all, vat_all = carried
    return (xp, xs,
            jnp.transpose(kat_all.reshape(depth, b, HA, DHA, keep), (0, 1, 4, 2, 3)),
            jnp.transpose(vat_all.reshape(depth, b, HA, DHA, keep), (0, 1, 4, 2, 3)),
            jnp.transpose(kt_all.reshape(depth, b, HB, 2, DHB, t), (0, 1, 5, 2, 3, 4)),
            v_all.reshape(depth, b, t, HB, 2 * DHB),
            pmem[0].reshape(depth, b, n_mem, HC, DHC), pmem[1].reshape(depth, b, n_mem, HC, DHC),
            stack(s_state, 0, (bd, sd, HA, DHA)), stack(s_state, 1, (bd, sd, HA, DHA)),
            stack(s_state, 2, (bd, sd, HB, 2, DHB)), stack(s_state, 3, (bd, sd, HB, 2 * DHB)))
```
